```python
import math
import jax
import jax.numpy as jnp
from jax import lax
import numpy as np

D_MODEL = 1024
BATCH = 4
SEQ = 4096
DEPTH = 4
DEC_BATCH = 128
DEC_SEQ = 4
PAST_LEN = 2048
PAGE_SIZE = 128

D_A = D_MODEL // 2
DH_A = 64
H_A = D_A // DH_A
MOBA_BLOCK = 256
MOBA_TOPK = 3
N_BUCKETS = 32
MAX_DISTANCE = 1024
D_B = D_MODEL // 2
DK = 128
DV = 128
H_B = D_B // DV
CONV_W = 4
CONV_DIM = 2 * H_B * DK + H_B * DV
DELTA_CHUNK = 64
PROMPT_Q_CHUNK = 32
SAMPLE_Q_CHUNK = 1
EPS = 1e-6
IN_SPLITS = (D_A, D_A, D_A, D_A, CONV_DIM, H_B * DV, H_B, H_B, D_MODEL, D_MODEL)
D_IN = 4 * D_A + CONV_DIM + H_B * DV + 2 * H_B + 2 * D_MODEL

kernel_name = 'moba_gdn_parallel_gated_decoder_step'


def rms_norm(x, w):
    xf = x.astype(jnp.float32)
    return (xf * lax.rsqrt(jnp.mean(xf * xf, axis=-1, keepdims=True) + EPS) * w).astype(x.dtype)


def l2_normalize(x):
    xf = x.astype(jnp.float32)
    return xf * lax.rsqrt(jnp.sum(xf * xf, axis=-1, keepdims=True) + EPS)


def rel_bucket(dist):
    n = jnp.maximum(dist, 0)
    max_exact = N_BUCKETS // 2
    nf = jnp.maximum(n, 1).astype(jnp.float32)
    large = max_exact + (jnp.log(nf / max_exact) / math.log(MAX_DISTANCE / max_exact)
                         * (N_BUCKETS - max_exact)).astype(jnp.int32)
    large = jnp.minimum(large, N_BUCKETS - 1)
    return jnp.where(n < max_exact, n, large)


def moba_attention(q, k, v, q_pos, rel_bias, q_chunk):
    b, t, h, dh = q.shape
    l = k.shape[1]
    nb = -(-l // MOBA_BLOCK)
    pad = ((0, 0), (0, nb * MOBA_BLOCK - l), (0, 0), (0, 0))
    k_blk = jnp.pad(k, pad).reshape(b, nb, MOBA_BLOCK, h, dh)
    v_blk = jnp.pad(v, pad).reshape(b, nb, MOBA_BLOCK, h, dh)
    k_mean = jnp.mean(k_blk.astype(jnp.float32), axis=2)
    n_top = min(MOBA_TOPK, nb)
    bias_t = rel_bias.T.astype(jnp.float32)
    b_idx = jnp.arange(b)[:, None, None, None]
    h_idx = jnp.arange(h)[None, :, None, None]
    offs = jnp.arange(MOBA_BLOCK, dtype=jnp.int32)
    scale = dh ** -0.5

    def one_chunk(args):
        qc, pc = args
        c = pc.shape[0]
        qf = qc.astype(jnp.float32)
        qb = pc // MOBA_BLOCK
        gate = jnp.einsum('bchd,bnhd->bhcn', qf, k_mean)
        is_past = jnp.arange(nb)[None, :] < qb[:, None]
        gate = jnp.where(is_past[None, None], gate, -jnp.inf)
        _, top = lax.top_k(gate, n_top)
        own = jnp.broadcast_to(qb[None, None, :, None], (b, h, c, 1)).astype(top.dtype)
        sel = jnp.concatenate([top, own], axis=-1)
        slot_ok = jnp.concatenate([jnp.arange(n_top)[None, :] < qb[:, None],
                                   jnp.ones((c, 1), bool)], axis=-1)
        k_sel = k_blk[b_idx, sel, :, h_idx]
        v_sel = v_blk[b_idx, sel, :, h_idx]
        logits = jnp.einsum('bchd,bhcsjd->bhcsj', qf, k_sel.astype(jnp.float32)) * scale
        dist = pc[None, None, :, None, None] - (sel[..., None] * MOBA_BLOCK + offs)
        bias = bias_t[h_idx[..., None], rel_bucket(dist)]
        mask = slot_ok[None, None, :, :, None] & (dist >= 0)
        logits = jnp.where(mask, logits + bias, -jnp.inf)
        p = jax.nn.softmax(logits.reshape(b, h, c, -1), axis=-1).reshape(logits.shape)
        return jnp.einsum('bhcsj,bhcsjd->bchd', p, v_sel.astype(jnp.float32)).astype(q.dtype)

    n = t // q_chunk
    qs = jnp.moveaxis(q.reshape(b, n, q_chunk, h, dh), 1, 0)
    ps = q_pos.reshape(n, q_chunk)
    out = lax.map(one_chunk, (qs, ps))
    return jnp.moveaxis(out, 0, 1).reshape(b, t, h, dh)


def gated_delta_rule(q, k, v, g, beta, s0, chunk):
    b, t, h, dk = q.shape
    dv = v.shape[-1]
    n = t // chunk

    def blocks(a):
        return jnp.moveaxis(a.astype(jnp.float32).reshape((b, n, chunk, h) + a.shape[3:]), 3, 1)

    q, k, v, g, beta = (blocks(a) for a in (q, k, v, g, beta))
    q = q * dk ** -0.5
    gc = jnp.cumsum(g, axis=-1)
    idx = jnp.arange(chunk)
    causal = idx[:, None] >= idx[None, :]
    decay = jnp.exp(jnp.where(causal, gc[..., :, None] - gc[..., None, :], -jnp.inf))
    kb = k * beta[..., None]
    lower = jnp.where(idx[:, None] > idx[None, :],
                      jnp.einsum('bhncd,bhnjd->bhncj', kb, k) * decay, 0.0)
    a_mat = lower + jnp.eye(chunk, dtype=jnp.float32)
    rhs = jnp.concatenate([v * beta[..., None], kb * jnp.exp(gc)[..., None]], axis=-1)
    sol = lax.linalg.triangular_solve(a_mat, rhs, left_side=True, lower=True, unit_diagonal=True)
    u, w = sol[..., :dv], sol[..., dv:]
    qk = jnp.einsum('bhncd,bhnjd->bhncj', q, k) * decay
    xs = tuple(jnp.moveaxis(a, 2, 0) for a in (q, k, u, w, qk, gc))

    def step(s, inp):
        qi, ki, ui, wi, qki, gi = inp
        v_new = ui - jnp.einsum('bhcd,bhde->bhce', wi, s)
        o = (jnp.einsum('bhcd,bhde->bhce', qi * jnp.exp(gi)[..., None], s)
             + jnp.einsum('bhcj,bhje->bhce', qki, v_new))
        g_last = gi[..., -1:]
        s = (s * jnp.exp(g_last)[..., None]
             + jnp.einsum('bhcd,bhce->bhde', ki * jnp.exp(g_last - gi)[..., None], v_new))
        return s, o

    s_final, o = lax.scan(step, s0.astype(jnp.float32), xs)
    o = jnp.moveaxis(jnp.moveaxis(o, 0, 2), 1, 3).reshape(b, t, h, dv)
    return o, s_final


def trunk_layer(x, c, k_past, v_past, s0, conv0, q_pos, q_chunk,
                rel_bias, w_ada, b_ada, g_pre, g_post, w_in, conv_w, a_log, dt_bias, g_onorm,
                w_pa, w_pb, w_out):
    bsz, t, _ = x.shape
    mod = jnp.dot(jax.nn.silu(c), w_ada) + b_ada
    shift, scale, gate = jnp.split(mod[:, None, :], 3, axis=-1)
    h = rms_norm(x, g_pre) * (1 + scale) + shift
    proj = jnp.dot(h, w_in)
    pts = np.cumsum(IN_SPLITS)[:-1].tolist()
    q_a, k_a, v_a, z_a, qkv_b, z_b, b_raw, a_raw, gate_a, gate_b = jnp.split(proj, pts, axis=-1)

    q_a = q_a.reshape(bsz, t, H_A, DH_A)
    k_a = k_a.reshape(bsz, t, H_A, DH_A)
    v_a = v_a.reshape(bsz, t, H_A, DH_A)
    k_all = k_a if k_past is None else jnp.concatenate([k_past.astype(k_a.dtype), k_a], axis=1)
    v_all = v_a if v_past is None else jnp.concatenate([v_past.astype(v_a.dtype), v_a], axis=1)
    o_a = moba_attention(q_a, k_all, v_all, q_pos, rel_bias, q_chunk).reshape(bsz, t, D_A)
    o_a = o_a * jax.nn.silu(z_a)

    u_pad = jnp.concatenate([conv0.astype(qkv_b.dtype), qkv_b], axis=1)
    u = u_pad[:, 0:t] * conv_w[0]
    for j in range(1, CONV_W):
        u = u + u_pad[:, j:j + t] * conv_w[j]
    u = jax.nn.silu(u)
    q_b, k_b, v_b = jnp.split(u, [H_B * DK, 2 * H_B * DK], axis=-1)
    q_b = l2_normalize(q_b.reshape(bsz, t, H_B, DK))
    k_b = l2_normalize(k_b.reshape(bsz, t, H_B, DK))
    v_b = v_b.reshape(bsz, t, H_B, DV)
    beta = jax.nn.sigmoid(b_raw.astype(jnp.float32))
    g = -jnp.exp(a_log.astype(jnp.float32)) * jax.nn.softplus(a_raw.astype(jnp.float32) + dt_bias)
    o_b, s_new = gated_delta_rule(q_b, k_b, v_b, g, beta, s0, math.gcd(t, DELTA_CHUNK))
    o_b = rms_norm(o_b, g_onorm).astype(x.dtype).reshape(bsz, t, D_B) * jax.nn.silu(z_b)

    merged = jax.nn.sigmoid(gate_a) * jnp.dot(o_a, w_pa) + jax.nn.sigmoid(gate_b) * jnp.dot(o_b, w_pb)
    y = rms_norm(jnp.dot(merged, w_out), g_post)
    conv_new = u_pad[:, u_pad.shape[1] - (CONV_W - 1):]
    return x + gate * y, k_a, v_a, s_new, conv_new


def setup_inputs(seed: int = 0) -> dict:
    key = jax.random.key(seed)
    ks = jax.random.split(key, 24)
    n_pages = PAST_LEN // PAGE_SIZE
    n_pool = (DEC_BATCH * n_pages * 5) // 4
    perm = jax.random.permutation(ks[0], n_pool)
    page_table = perm[:DEC_BATCH * n_pages].reshape(DEC_BATCH, n_pages).astype(jnp.int32)
    f32 = jnp.float32
    x_prompt = jax.random.normal(ks[1], (BATCH, SEQ, D_MODEL), f32)
    x_sample = jax.random.normal(ks[2], (DEC_BATCH, DEC_SEQ, D_MODEL), f32)
    c_prompt = jax.random.normal(ks[3], (BATCH, D_MODEL), f32)
    c_sample = jax.random.normal(ks[4], (DEC_BATCH, D_MODEL), f32)
    cache_k = jax.random.normal(ks[5], (DEPTH, n_pool, PAGE_SIZE, H_A, DH_A), f32)
    cache_v = jax.random.normal(ks[6], (DEPTH, n_pool, PAGE_SIZE, H_A, DH_A), f32)
    state_delta = jax.random.normal(ks[7], (DEPTH, DEC_BATCH, H_B, DK, DV), f32) * DK ** -0.5
    state_conv = jax.random.normal(ks[8], (DEPTH, DEC_BATCH, CONV_W - 1, CONV_DIM), f32)
    rel_bias = jax.random.normal(ks[9], (N_BUCKETS, H_A), f32) * 0.5
    w_ada = jax.random.normal(ks[10], (DEPTH, D_MODEL, 3 * D_MODEL), f32) * D_MODEL ** -0.5
    b_ada = jax.random.normal(ks[11], (DEPTH, 3 * D_MODEL), f32) * 0.01
    g_pre = 1.0 + 0.05 * jax.random.normal(ks[12], (DEPTH, D_MODEL), f32)
    g_post = 1.0 + 0.05 * jax.random.normal(ks[13], (DEPTH, D_MODEL), f32)
    alpha_start = 4 * D_A + CONV_DIM + H_B * DV + H_B
    col_scale = jnp.ones((D_IN,), f32).at[alpha_start:alpha_start + H_B].set(0.1)
    w_in = jax.random.normal(ks[14], (DEPTH, D_MODEL, D_IN), f32) * D_MODEL ** -0.5 * col_scale
    conv_w = jax.random.normal(ks[15], (DEPTH, CONV_W, CONV_DIM), f32) * CONV_W ** -0.5
    a_log = jnp.log(jax.random.uniform(ks[16], (DEPTH, H_B), f32, minval=1.0, maxval=16.0))
    dt = jnp.exp(jax.random.uniform(ks[17], (DEPTH, H_B), f32,
                                    minval=math.log(1e-3), maxval=math.log(0.1)))
    dt_bias = dt + jnp.log(-jnp.expm1(-dt))
    g_onorm = 1.0 + 0.05 * jax.random.normal(ks[18], (DEPTH, DV), f32)
    w_pa = jax.random.normal(ks[19], (DEPTH, D_A, D_MODEL), f32) * D_A ** -0.5
    w_pb = jax.random.normal(ks[20], (DEPTH, D_B, D_MODEL), f32) * D_B ** -0.5
    w_out = jax.random.normal(ks[21], (DEPTH, D_MODEL, D_MODEL), f32) * D_MODEL ** -0.5
    return {'x_prompt': x_prompt, 'x_sample': x_sample, 'c_prompt': c_prompt, 'c_sample': c_sample,
            'cache_k': cache_k, 'cache_v': cache_v, 'state_delta': state_delta, 'state_conv': state_conv,
            'page_table': page_table, 'rel_bias': rel_bias, 'w_ada': w_ada, 'b_ada': b_ada,
            'g_pre': g_pre, 'g_post': g_post, 'w_in': w_in, 'conv_w': conv_w, 'a_log': a_log,
            'dt_bias': dt_bias, 'g_onorm': g_onorm, 'w_pa': w_pa, 'w_pb': w_pb, 'w_out': w_out}


def reference(x_prompt, x_sample, c_prompt, c_sample, cache_k, cache_v, state_delta, state_conv,
              page_table, rel_bias, w_ada, b_ada, g_pre, g_post, w_in, conv_w, a_log, dt_bias,
              g_onorm, w_pa, w_pb, w_out):
    bsz, seq, _ = x_prompt.shape
    dec_b, dec_t, _ = x_sample.shape
    n_pages = page_table.shape[1]
    past_len = n_pages * PAGE_SIZE
    pos_prompt = jnp.arange(seq, dtype=jnp.int32)
    pos_sample = past_len + jnp.arange(dec_t, dtype=jnp.int32)
    s0_prompt = jnp.zeros((bsz, H_B, DK, DV), jnp.float32)
    conv0_prompt = jnp.zeros((bsz, CONV_W - 1, CONV_DIM), x_prompt.dtype)
    hp, hs = x_prompt, x_sample
    kp_l, vp_l, ks_l, vs_l, sp_l, ss_l, cp_l, cs_l = [], [], [], [], [], [], [], []
    for l in range(DEPTH):
        lw = (rel_bias, w_ada[l], b_ada[l], g_pre[l], g_post[l], w_in[l], conv_w[l], a_log[l],
              dt_bias[l], g_onorm[l], w_pa[l], w_pb[l], w_out[l])
        hp, kp, vp, sp, cp = trunk_layer(hp, c_prompt, None, None, s0_prompt, conv0_prompt, pos_prompt,
                                         math.gcd(seq, PROMPT_Q_CHUNK), *lw)
        k_past = cache_k[l, page_table].reshape(dec_b, past_len, H_A, DH_A)
        v_past = cache_v[l, page_table].reshape(dec_b, past_len, H_A, DH_A)
        hs, ksm, vsm, ssm, csm = trunk_layer(hs, c_sample, k_past, v_past, state_delta[l], state_conv[l],
                                             pos_sample, SAMPLE_Q_CHUNK, *lw)
        kp_l.append(kp); vp_l.append(vp); ks_l.append(ksm); vs_l.append(vsm)
        sp_l.append(sp); ss_l.append(ssm); cp_l.append(cp); cs_l.append(csm)
    return (hp, hs, jnp.stack(kp_l), jnp.stack(vp_l), jnp.stack(ks_l), jnp.stack(vs_l),
            jnp.stack(sp_l), jnp.stack(ss_l), jnp.stack(cp_l), jnp.stack(cs_l))
```

```python
import functools
import math

import jax
import jax.numpy as jnp
from jax import lax
from jax.experimental import pallas as pl
from jax.experimental.pallas import tpu as pltpu

f32 = jnp.float32
bf16 = jnp.bfloat16
HIGHEST = lax.Precision.HIGHEST

D_MODEL = 1024
D_A = D_MODEL // 2
DH_A = 64
H_A = D_A // DH_A
MOBA_BLOCK = 256
MOBA_TOPK = 3
N_BUCKETS = 32
MAX_DISTANCE = 1024
D_B = D_MODEL // 2
DK = 128
DV = 128
H_B = D_B // DV
CONV_W = 4
CONV_DIM = 2 * H_B * DK + H_B * DV
DELTA_CHUNK = 64
PAGE_SIZE = 128
EPS = 1e-6

LANES = 128
SUBLANES = 8
NEG = -1e30
TOKEN_TILE = MOBA_BLOCK
SAMPLE_PAD = SUBLANES
NEAR_TILES = MAX_DISTANCE // MOBA_BLOCK + 1
W_MAIN = 4 * D_A + CONV_DIM + H_B * DV
W_BA = W_MAIN
W_GATES = W_MAIN + LANES
W_TOTAL = W_GATES + 2 * D_MODEL
VMEM_LIMIT = 48 * 1024 * 1024


def _cparams(*sem):
    return pltpu.CompilerParams(dimension_semantics=sem, vmem_limit_bytes=VMEM_LIMIT)


def _dot(a, b):
    return jnp.dot(a.astype(bf16), b.astype(bf16), preferred_element_type=f32)


def _dot_nt(a, b):
    return lax.dot_general(a.astype(bf16), b.astype(bf16), (((1,), (1,)), ((), ())), preferred_element_type=f32)


def _dot_tn(a, b):
    return lax.dot_general(a.astype(bf16), b.astype(bf16), (((0,), (0,)), ((), ())), preferred_element_type=f32)


def _silu(x):
    return x * jax.nn.sigmoid(x)


def _softplus(x):
    return jnp.maximum(x, 0.0) + jnp.log(1.0 + jnp.exp(-jnp.abs(x)))


def _div_pow2(x, n):
    return lax.shift_right_logical(x, int(math.log2(n)))


def _rel_bucket(dist):
    n = jnp.maximum(dist, 0)
    max_exact = N_BUCKETS // 2
    nf = jnp.maximum(n, 1).astype(f32)
    large = max_exact + (jnp.log(nf / max_exact) / math.log(MAX_DISTANCE / max_exact)
                         * (N_BUCKETS - max_exact)).astype(jnp.int32)
    large = jnp.minimum(large, N_BUCKETS - 1)
    return jnp.where(n < max_exact, n, large)


def _bias_prompt_kernel(rb_ref, o_ref):
    d = pl.program_id(0)
    h = pl.program_id(1)
    r = lax.broadcasted_iota(jnp.int32, (MOBA_BLOCK, MOBA_BLOCK), 0)
    c = lax.broadcasted_iota(jnp.int32, (MOBA_BLOCK, MOBA_BLOCK), 1)
    dist = d * MOBA_BLOCK + c - r
    bucket = _rel_bucket(dist)
    acc = jnp.zeros((MOBA_BLOCK, MOBA_BLOCK), f32)
    for b in range(N_BUCKETS):
        acc = jnp.where(bucket == b, rb_ref[b, h], acc)
    o_ref[...] = jnp.where(dist >= 0, acc, NEG)


def _bias_prompt(rel_bias):
    return pl.pallas_call(
        _bias_prompt_kernel,
        out_shape=jax.ShapeDtypeStruct((NEAR_TILES, H_A, MOBA_BLOCK, MOBA_BLOCK), f32),
        grid=(NEAR_TILES, H_A),
        in_specs=[pl.BlockSpec(memory_space=pltpu.SMEM)],
        out_specs=pl.BlockSpec((None, None, MOBA_BLOCK, MOBA_BLOCK), lambda d, h: (d, h, 0, 0)),
        compiler_params=_cparams("arbitrary", "arbitrary"),
        name="bias_prompt",
    )(rel_bias)


def _bias_sample_kernel(rb_ref, o_ref, far_ref, *, past_len, dec_t):
    rows = H_A * SAMPLE_PAD
    width = past_len + LANES
    row = lax.broadcasted_iota(jnp.int32, (rows, width), 0)
    lane = lax.broadcasted_iota(jnp.int32, (rows, width), 1)
    rowh = _div_pow2(row, SAMPLE_PAD)
    tq = jnp.minimum(row - rowh * SAMPLE_PAD, dec_t - 1)
    tn = lane - past_len
    is_new = lane >= past_len
    dist = jnp.where(is_new, tq - tn, past_len + tq - lane)
    valid = jnp.logical_or(jnp.logical_not(is_new), jnp.logical_and(tn <= tq, tn < dec_t))
    bucket = _rel_bucket(dist)
    far_bucket = _rel_bucket(jnp.full((H_A, LANES), MAX_DISTANCE, jnp.int32))
    far_row = lax.broadcasted_iota(jnp.int32, (H_A, LANES), 0)

    def body(b, carry):
        acc, far = carry
        for h in range(H_A):
            v = rb_ref[b, h]
            acc = jnp.where(jnp.logical_and(bucket == b, rowh == h), v, acc)
            far = jnp.where(jnp.logical_and(far_bucket == b, far_row == h), v, far)
        return acc, far

    acc, far = lax.fori_loop(0, N_BUCKETS, body, (jnp.zeros((rows, width), f32), jnp.zeros((H_A, LANES), f32)))
    o_ref[...] = jnp.where(valid, acc, NEG)
    far_ref[...] = far


def _bias_sample(rel_bias, past_len, dec_t):
    return pl.pallas_call(
        functools.partial(_bias_sample_kernel, past_len=past_len, dec_t=dec_t),
        out_shape=(jax.ShapeDtypeStruct((H_A * SAMPLE_PAD, past_len + LANES), f32),
                   jax.ShapeDtypeStruct((H_A, LANES), f32)),
        in_specs=[pl.BlockSpec(memory_space=pltpu.SMEM)],
        name="bias_sample",
    )(rel_bias)


def _mod_kernel(c_ref, w_ref, b_ref, o_ref):
    o_ref[...] = _dot(_silu(c_ref[...]), w_ref[...]) + b_ref[...]


def _modulation(c_all, w_ada, b_ada):
    depth = w_ada.shape[0]
    rows = c_all.shape[0]
    return pl.pallas_call(
        _mod_kernel,
        out_shape=jax.ShapeDtypeStruct((depth, rows, 3 * D_MODEL), f32),
        grid=(depth, 3),
        in_specs=[pl.BlockSpec((rows, D_MODEL), lambda l, j: (0, 0)),
                  pl.BlockSpec((None, D_MODEL, D_MODEL), lambda l, j: (l, 0, j)),
                  pl.BlockSpec((None, 1, D_MODEL), lambda l, j: (l, 0, j))],
        out_specs=pl.BlockSpec((None, rows, D_MODEL), lambda l, j: (l, 0, j)),
        compiler_params=_cparams("arbitrary", "arbitrary"),
        name="modulation",
    )(c_all, w_ada, b_ada.reshape(depth, 1, 3 * D_MODEL))


def _inproj_kernel(x_ref, shift_ref, scale_ref, gpre_ref, w_ref, *outs, prompt):
    x = x_ref[...]
    h = x * lax.rsqrt(jnp.mean(x * x, axis=-1, keepdims=True) + EPS) * gpre_ref[...]
    hb = (h * (1.0 + scale_ref[...]) + shift_ref[...]).astype(bf16)

    def proj(lo, hi):
        return jnp.dot(hb, w_ref[:, lo:hi], preferred_element_type=f32)

    q = proj(0, D_A)
    k = proj(D_A, 2 * D_A)
    v = proj(2 * D_A, 3 * D_A)
    if prompt:
        k_ref, v_ref, kb_ref, qt_ref, vt_ref, km_ref, za_ref, xb_ref, zb_ref, ba_ref, ga_ref, gb_ref = outs
        vt = v.T
        k_ref[...] = k.T
        v_ref[...] = vt
        kb_ref[...] = k.astype(bf16)
        qt_ref[...] = q.T
        vt_ref[...] = vt.astype(bf16)
        km_ref[...] = jnp.mean(k, axis=0, keepdims=True)
    else:
        q_ref, k_ref, v_ref, za_ref, xb_ref, zb_ref, ba_ref, ga_ref, gb_ref = outs
        q_ref[...] = q
        k_ref[...] = k
        v_ref[...] = v
    za_ref[...] = proj(3 * D_A, 4 * D_A)
    xb_ref[...] = proj(4 * D_A, 4 * D_A + CONV_DIM)
    zb_ref[...] = proj(4 * D_A + CONV_DIM, W_MAIN)
    ba_ref[...] = proj(W_BA, W_BA + LANES)
    ga_ref[...] = proj(W_GATES, W_GATES + D_MODEL)
    gb_ref[...] = proj(W_GATES + D_MODEL, W_TOTAL)


def _inproj(x, shift, scale, g_pre, w_cat, *, prompt, tiles_per_seq=None):
    n = x.shape[0]
    tm = TOKEN_TILE
    nt = n // tm
    row = lambda i: (i, 0)
    if prompt:
        mod_spec = pl.BlockSpec((None, 1, D_MODEL), lambda i: (i // tiles_per_seq, 0, 0))
    else:
        mod_spec = pl.BlockSpec((tm, D_MODEL), row)
    tok = lambda w, dt=f32: (jax.ShapeDtypeStruct((n, w), dt), pl.BlockSpec((tm, w), row))
    common = [tok(D_A), tok(CONV_DIM), tok(D_B), tok(LANES), tok(D_MODEL), tok(D_MODEL)]
    if prompt:
        seq_t = (jax.ShapeDtypeStruct((nt // tiles_per_seq, D_A, tiles_per_seq * tm), f32),
                 pl.BlockSpec((None, D_A, tm), lambda i: (i // tiles_per_seq, 0, i % tiles_per_seq)))
        outs = [seq_t, seq_t, tok(D_A, bf16),
                (jax.ShapeDtypeStruct((nt, D_A, tm), f32), pl.BlockSpec((None, D_A, tm), lambda i: (i, 0, 0))),
                (jax.ShapeDtypeStruct((nt, D_A, tm), bf16), pl.BlockSpec((None, D_A, tm), lambda i: (i, 0, 0))),
                (jax.ShapeDtypeStruct((nt, 1, D_A), f32), pl.BlockSpec((None, 1, D_A), lambda i: (i, 0, 0)))] + common
    else:
        outs = [tok(D_A), tok(D_A), tok(D_A)] + common
    return pl.pallas_call(
        functools.partial(_inproj_kernel, prompt=prompt),
        out_shape=tuple(o[0] for o in outs),
        grid=(nt,),
        in_specs=[pl.BlockSpec((tm, D_MODEL), row), mod_spec, mod_spec,
                  pl.BlockSpec((1, D_MODEL), lambda i: (0, 0)),
                  pl.BlockSpec((D_MODEL, W_TOTAL), lambda i: (0, 0), pipeline_mode=pl.Buffered(1))],
        out_specs=tuple(o[1] for o in outs),
        compiler_params=_cparams("arbitrary"),
        name="inproj_prompt" if prompt else "inproj_sample",
    )(x, shift, scale, g_pre.reshape(1, D_MODEL), w_cat)


def _moba_prompt_kernel(far_ref, qt_ref, kb_ref, vt_ref, km_ref, bias_ref, o_ref, sel_scr, qm_scr):
    hp = pl.program_id(1)
    i = pl.program_id(2)
    nb = km_ref.shape[0]
    qt = qt_ref[...]
    km = km_ref[...]
    blk = lax.broadcasted_iota(jnp.int32, (nb, MOBA_BLOCK), 0)
    feat = lax.broadcasted_iota(jnp.int32, (2 * DH_A, MOBA_BLOCK), 0)
    for hh in range(2):
        qm = jnp.where(jnp.logical_and(feat >= hh * DH_A, feat < (hh + 1) * DH_A), qt, 0.0)
        gate = jnp.dot(km, qm, precision=HIGHEST, preferred_element_type=f32)
        g = jnp.where(blk < i, gate, -jnp.inf)
        chosen = blk == i
        blkf = blk.astype(f32)
        for _ in range(MOBA_TOPK):
            m = jnp.max(g, axis=0, keepdims=True)
            first = jnp.min(jnp.where(g == m, blkf, float(nb)), axis=0, keepdims=True)
            hit = jnp.logical_and(blkf == first, m > -jnp.inf)
            chosen = jnp.logical_or(chosen, hit)
            g = jnp.where(hit, -jnp.inf, g)
        sel_scr[hh] = jnp.where(chosen, 0.0, NEG)
        qm_scr[hh] = (qm * DH_A ** -0.5).astype(bf16)

    def step(j, carry, bias_of):
        kj = kb_ref[j]
        vj = vt_ref[j]
        new = []
        for hh in range(2):
            m, l, acc = carry[hh]
            s = jnp.dot(kj, qm_scr[hh], preferred_element_type=f32)
            s = s + bias_of(j, hh) + sel_scr[hh, pl.ds(j, 1), :]
            m_new = jnp.maximum(m, jnp.max(s, axis=0, keepdims=True))
            alpha = jnp.exp(m - m_new)
            p = jnp.exp(s - m_new)
            l = alpha * l + jnp.sum(p, axis=0, keepdims=True)
            acc = alpha * acc + jnp.dot(vj[hh * DH_A:(hh + 1) * DH_A, :], p.astype(bf16),
                                        preferred_element_type=f32)
            new.append((m_new, l, acc))
        return tuple(new)

    init = tuple((jnp.full((1, MOBA_BLOCK), NEG, f32), jnp.zeros((1, MOBA_BLOCK), f32),
                  jnp.zeros((DH_A, MOBA_BLOCK), f32)) for _ in range(2))
    n_far = jnp.maximum(i - (NEAR_TILES - 1), 0)
    carry = lax.fori_loop(0, n_far, lambda j, c: step(j, c, lambda j, hh: far_ref[hp * 2 + hh]), init)
    carry = lax.fori_loop(n_far, i + 1, lambda j, c: step(j, c, lambda j, hh: bias_ref[i - j, hh]), carry)
    for hh in range(2):
        _, l, acc = carry[hh]
        o_ref[hh * DH_A:(hh + 1) * DH_A, :] = acc / l


def _moba_prompt(qt, kb, vt, kmean, bias, far):
    bsz, nb = kb.shape[0], kb.shape[1]
    hd = 2 * DH_A
    return pl.pallas_call(
        _moba_prompt_kernel,
        out_shape=jax.ShapeDtypeStruct((bsz * nb, D_A, MOBA_BLOCK), f32),
        grid=(bsz, H_A // 2, nb),
        in_specs=[pl.BlockSpec(memory_space=pltpu.SMEM),
                  pl.BlockSpec((None, hd, MOBA_BLOCK), lambda b, hp, i: (b * nb + i, hp, 0)),
                  pl.BlockSpec((None, nb, MOBA_BLOCK, hd), lambda b, hp, i: (b, 0, 0, hp)),
                  pl.BlockSpec((None, nb, hd, MOBA_BLOCK), lambda b, hp, i: (b, 0, hp, 0)),
                  pl.BlockSpec((None, nb, hd), lambda b, hp, i: (b, 0, hp)),
                  pl.BlockSpec((NEAR_TILES, 2, MOBA_BLOCK, MOBA_BLOCK), lambda b, hp, i: (0, hp, 0, 0))],
        out_specs=pl.BlockSpec((None, hd, MOBA_BLOCK), lambda b, hp, i: (b * nb + i, hp, 0)),
        scratch_shapes=[pltpu.VMEM((2, nb, MOBA_BLOCK), f32), pltpu.VMEM((2, hd, MOBA_BLOCK), bf16)],
        compiler_params=_cparams("arbitrary", "arbitrary", "arbitrary"),
        name="moba_prompt",
    )(far, qt, kb, vt, kmean, bias)


def _moba_sample_kernel(pt_ref, q_ref, kn_ref, vn_ref, bias_ref, *refs, n_pages):
    del pt_ref
    kp = refs[:n_pages]
    vp = refs[n_pages:2 * n_pages]
    o_ref = refs[2 * n_pages]
    rows = H_A * SAMPLE_PAD
    ppb = MOBA_BLOCK // PAGE_SIZE
    n_blocks = n_pages // ppb
    rowh = _div_pow2(lax.broadcasted_iota(jnp.int32, (rows, D_A), 0), SAMPLE_PAD)
    laneh = _div_pow2(lax.broadcasted_iota(jnp.int32, (rows, D_A), 1), DH_A)
    own = rowh == laneh
    qe = jnp.where(own, jnp.concatenate([q_ref[...]] * H_A, axis=0), 0.0)
    qeb = (qe * DH_A ** -0.5).astype(bf16)

    lane_f = lax.broadcasted_iota(jnp.int32, (D_A, LANES), 1)
    kmean = jnp.zeros((D_A, LANES), f32)
    for j in range(n_blocks):
        ksum = kp[j * ppb][...]
        for p in range(1, ppb):
            ksum = ksum + kp[j * ppb + p][...]
        kmean = jnp.where(lane_f == j, jnp.sum(ksum, axis=1, keepdims=True) / MOBA_BLOCK, kmean)
    gate = jnp.dot(qe, kmean, precision=HIGHEST, preferred_element_type=f32)
    lane_r = lax.broadcasted_iota(jnp.int32, (rows, LANES), 1)
    lane_rf = lane_r.astype(f32)
    g = jnp.where(lane_r < n_blocks, gate, -jnp.inf)
    chosen = jnp.zeros((rows, LANES), jnp.bool_)
    for _ in range(MOBA_TOPK):
        m = jnp.max(g, axis=1, keepdims=True)
        first = jnp.min(jnp.where(g == m, lane_rf, float(LANES)), axis=1, keepdims=True)
        hit = lane_rf == first
        chosen = jnp.logical_or(chosen, hit)
        g = jnp.where(hit, -jnp.inf, g)
    sel = jnp.where(chosen, 0.0, NEG)

    pad_rows = LANES - SAMPLE_PAD
    kn = jnp.concatenate([kn_ref[...], jnp.zeros((pad_rows, D_A), f32)], axis=0)
    vn = jnp.concatenate([vn_ref[...], jnp.zeros((pad_rows, D_A), f32)], axis=0)
    s = []
    for p in range(n_pages):
        sp = _dot(qeb, kp[p][...]) + bias_ref[:, p * PAGE_SIZE:(p + 1) * PAGE_SIZE]
        s.append(sp + sel[:, p // ppb:p // ppb + 1])
    s.append(_dot_nt(qeb, kn) + bias_ref[:, n_pages * PAGE_SIZE:n_pages * PAGE_SIZE + LANES])
    m = s[0]
    for sp in s[1:]:
        m = jnp.maximum(m, sp)
    m = jnp.max(m, axis=-1, keepdims=True)
    l = jnp.zeros((rows, 1), f32)
    acc = jnp.zeros((rows, D_A), f32)
    for p in range(n_pages + 1):
        e = jnp.exp(s[p] - m)
        l = l + jnp.sum(e, axis=-1, keepdims=True)
        acc = acc + (_dot_nt(e, vp[p][...]) if p < n_pages else _dot(e, vn))
    o = jnp.where(own, acc / l, 0.0)
    out = o[0:SAMPLE_PAD]
    for h in range(1, H_A):
        out = out + o[h * SAMPLE_PAD:(h + 1) * SAMPLE_PAD]
    o_ref[...] = out


def _moba_sample(q, kn, vn, cache_k, cache_v, page_table, bias, layer):
    nseq, n_pages = page_table.shape
    assert (n_pages * PAGE_SIZE) % MOBA_BLOCK == 0 and (n_pages * PAGE_SIZE) // MOBA_BLOCK >= MOBA_TOPK
    seq = pl.BlockSpec((None, SAMPLE_PAD, D_A), lambda b, pt: (b, 0, 0))
    page = lambda p: pl.BlockSpec((None, None, D_A, PAGE_SIZE), lambda b, pt: (layer, pt[b, p], 0, 0))
    grid_spec = pltpu.PrefetchScalarGridSpec(
        num_scalar_prefetch=1,
        grid=(nseq,),
        in_specs=[seq, seq, seq, pl.BlockSpec(bias.shape, lambda b, pt: (0, 0))]
        + [page(p) for p in range(n_pages)] * 2,
        out_specs=seq,
    )
    return pl.pallas_call(
        functools.partial(_moba_sample_kernel, n_pages=n_pages),
        out_shape=jax.ShapeDtypeStruct((nseq, SAMPLE_PAD, D_A), f32),
        grid_spec=grid_spec,
        compiler_params=_cparams("arbitrary"),
        name="moba_sample",
    )(page_table, q, kn, vn, bias, *([cache_k] * n_pages), *([cache_v] * n_pages))


def _gdn_kernel(x_ref, ba_ref, z_ref, conv0_ref, s0_ref, cw_ref, avec_ref, dtvec_ref, gon_ref,
                o_ref, s_ref, xs_scr, *, chunk, n_valid):
    t = pl.program_id(1)
    nb, tt = x_ref.shape[0], x_ref.shape[1]

    @pl.when(t == 0)
    def _():
        s_ref[...] = s0_ref[...]
        xs_scr[:, 0:SUBLANES, :] = conv0_ref[...]

    xs_scr[:, SUBLANES:SUBLANES + tt, :] = x_ref[...]

    ci = lax.broadcasted_iota(jnp.int32, (chunk, chunk), 0)
    cj = lax.broadcasted_iota(jnp.int32, (chunk, chunk), 1)
    causal = ci >= cj
    strict = ci > cj
    tri = causal.astype(f32)
    eye = (ci == cj).astype(f32)
    cw = cw_ref[...]
    neg_a = -jnp.exp(avec_ref[...])
    live = lax.broadcasted_iota(jnp.int32, (chunk, LANES), 0)
    merges = []
    for b in range(int(math.log2(chunk))):
        same_pair = lax.shift_right_logical(ci, b + 1) == lax.shift_right_logical(cj, b + 1)
        row_hi = jnp.bitwise_and(lax.shift_right_logical(ci, b), 1) == 1
        col_lo = jnp.bitwise_and(lax.shift_right_logical(cj, b), 1) == 0
        merges.append(jnp.logical_and(same_pair, jnp.logical_and(row_hi, col_lo)))

    for n in range(nb):
        for c in range(tt // chunk):
            r0 = c * chunk
            base = SUBLANES - (CONV_W - 1) + r0
            u = xs_scr[n, base:base + chunk, :] * cw[0:1]
            for j in range(1, CONV_W):
                u = u + xs_scr[n, base + j:base + j + chunk, :] * cw[j:j + 1]
            u = _silu(u)
            ba = ba_ref[n, r0:r0 + chunk, :]
            beta = jax.nn.sigmoid(ba)
            g = neg_a * _softplus(ba + dtvec_ref[...])
            if n_valid < tt:
                keep = live + r0 < n_valid
                beta = jnp.where(keep, beta, 0.0)
                g = jnp.where(keep, g, 0.0)
            gcum = jnp.dot(tri, g, precision=HIGHEST, preferred_element_type=f32)
            for h in range(H_B):
                q = u[:, h * DK:(h + 1) * DK]
                k = u[:, H_B * DK + h * DK:H_B * DK + (h + 1) * DK]
                v = u[:, 2 * H_B * DK + h * DV:2 * H_B * DK + (h + 1) * DV]
                q = q * lax.rsqrt(jnp.sum(q * q, axis=-1, keepdims=True) + EPS) * DK ** -0.5
                k = k * lax.rsqrt(jnp.sum(k * k, axis=-1, keepdims=True) + EPS)
                bcol = beta[:, h:h + 1]
                gcol = gcum[:, H_B + h:H_B + h + 1]
                grow = jnp.sum(eye * gcol, axis=0, keepdims=True)
                decay = jnp.exp(jnp.where(causal, gcol - grow, NEG))
                kb = k * bcol
                lower = jnp.where(strict, _dot_nt(kb, k) * decay, 0.0)
                inv = eye
                for merge in merges:
                    inv = inv - _dot(_dot(inv, jnp.where(merge, lower, 0.0)), inv)
                sol = _dot(inv, jnp.concatenate([v * bcol, kb * jnp.exp(gcol)], axis=1))
                uu, ww = sol[:, :DV], sol[:, DV:]
                qk = jnp.where(causal, _dot_nt(q, k) * decay, 0.0)
                state = s_ref[n, h]
                v_new = uu - _dot(ww, state)
                o = _dot(q * jnp.exp(gcol), state) + _dot(qk, v_new)
                g_last = gcol[chunk - 1:chunk, :]
                s_ref[n, h] = state * jnp.exp(g_last) + _dot_tn(k * jnp.exp(g_last - gcol), v_new)
                o = o * lax.rsqrt(jnp.mean(o * o, axis=-1, keepdims=True) + EPS) * gon_ref[...]
                o_ref[n, r0:r0 + chunk, h * DV:(h + 1) * DV] = o * _silu(z_ref[n, r0:r0 + chunk, h * DV:(h + 1) * DV])

    xs_scr[:, 0:SUBLANES, :] = xs_scr[:, tt:tt + SUBLANES, :]


def _gdn(xb, ba, zb, conv0, s0, layer, conv_w, a_log, dt_bias, g_onorm, *, nb, tt, chunk, n_valid):
    nseq, t_len = xb.shape[0], xb.shape[1]
    lane_vec = lambda v: jnp.zeros((1, LANES), f32).at[0, H_B:2 * H_B].set(v)
    tok = lambda w: pl.BlockSpec((nb, tt, w), lambda g, t: (g, t, 0))
    const = lambda shape: pl.BlockSpec(shape, lambda g, t: (0,) * len(shape))
    return pl.pallas_call(
        functools.partial(_gdn_kernel, chunk=chunk, n_valid=n_valid),
        out_shape=(jax.ShapeDtypeStruct((nseq, t_len, D_B), f32),
                   jax.ShapeDtypeStruct((nseq, H_B, DK, DV), f32)),
        grid=(nseq // nb, t_len // tt),
        in_specs=[tok(CONV_DIM), tok(LANES), tok(D_B),
                  pl.BlockSpec((None, nb, SUBLANES, CONV_DIM), lambda g, t: (layer, g, 0, 0)),
                  pl.BlockSpec((None, nb, H_B, DK, DV), lambda g, t: (layer, g, 0, 0, 0)),
                  const((CONV_W, CONV_DIM)), const((1, LANES)), const((1, LANES)), const((1, DV))],
        out_specs=(tok(D_B), pl.BlockSpec((nb, H_B, DK, DV), lambda g, t: (g, 0, 0, 0))),
        scratch_shapes=[pltpu.VMEM((nb, tt + SUBLANES, CONV_DIM), f32)],
        compiler_params=_cparams("arbitrary", "arbitrary"),
        name="gdn_%d" % tt,
    )(xb, ba, zb, conv0, s0, conv_w, lane_vec(a_log), lane_vec(dt_bias), g_onorm.reshape(1, DV))


def _outproj_kernel(oa_ref, za_ref, ob_ref, ga_ref, gb_ref, x_ref, gate_ref, wpa_ref, wpb_ref, wout_ref,
                    gpost_ref, y_ref, *, prompt):
    oa = oa_ref[...].T if prompt else oa_ref[...]
    oa = oa * _silu(za_ref[...])
    merged = (jax.nn.sigmoid(ga_ref[...]) * _dot(oa, wpa_ref[...])
              + jax.nn.sigmoid(gb_ref[...]) * _dot(ob_ref[...], wpb_ref[...]))
    y = _dot(merged, wout_ref[...])
    y = y * lax.rsqrt(jnp.mean(y * y, axis=-1, keepdims=True) + EPS) * gpost_ref[...]
    y_ref[...] = x_ref[...] + gate_ref[...] * y


def _outproj(oa, za, ob, ga, gb, x, gate, w_pa, w_pb, w_out, g_post, *, prompt, tiles_per_seq=None):
    n = x.shape[0]
    tm = TOKEN_TILE
    row = lambda i: (i, 0)
    tok = lambda w: pl.BlockSpec((tm, w), row)
    const = lambda shape: pl.BlockSpec(shape, lambda i: (0, 0))
    if prompt:
        oa_spec = pl.BlockSpec((None, D_A, tm), lambda i: (i, 0, 0))
        gate_spec = pl.BlockSpec((None, 1, D_MODEL), lambda i: (i // tiles_per_seq, 0, 0))
    else:
        oa_spec = tok(D_A)
        gate_spec = tok(D_MODEL)
    return pl.pallas_call(
        functools.partial(_outproj_kernel, prompt=prompt),
        out_shape=jax.ShapeDtypeStruct((n, D_MODEL), f32),
        grid=(n // tm,),
        in_specs=[oa_spec, tok(D_A), tok(D_B), tok(D_MODEL), tok(D_MODEL), tok(D_MODEL), gate_spec,
                  const((D_A, D_MODEL)), const((D_B, D_MODEL)), const((D_MODEL, D_MODEL)), const((1, D_MODEL))],
        out_specs=tok(D_MODEL),
        compiler_params=_cparams("arbitrary"),
        name="outproj_prompt" if prompt else "outproj_sample",
    )(oa, za, ob, ga, gb, x, gate, w_pa, w_pb, w_out, g_post.reshape(1, D_MODEL))


def _pack_w_in(w_in):
    depth = w_in.shape[0]
    n_ba = 2 * H_B
    ba = jnp.zeros((depth, D_MODEL, LANES), w_in.dtype).at[:, :, :n_ba].set(w_in[:, :, W_MAIN:W_MAIN + n_ba])
    return jnp.concatenate([w_in[:, :, :W_MAIN], ba, w_in[:, :, W_MAIN + n_ba:]], axis=-1).astype(bf16)


def _prompt_layer(x, mod, lw, bias, far, zero_conv, zero_state):
    bsz, t_len, _ = x.shape
    n = bsz * t_len
    nb = t_len // MOBA_BLOCK
    g_pre, g_post, w_cat, conv_w, a_log, dt_bias, g_onorm, w_pa, w_pb, w_out = lw
    shift, scale, gate = (mod[:, None, j * D_MODEL:(j + 1) * D_MODEL] for j in range(3))
    xf = x.reshape(n, D_MODEL)
    k, v, kb, qt, vt, km, za, xb, zb, ba, ga, gb = _inproj(xf, shift, scale, g_pre, w_cat, prompt=True,
                                                             tiles_per_seq=nb)
    ot = _moba_prompt(qt, kb.reshape(bsz, nb, MOBA_BLOCK, D_A), vt.reshape(bsz, nb, D_A, MOBA_BLOCK),
                      km.reshape(bsz, nb, D_A), bias, far)
    xb3 = xb.reshape(bsz, t_len, CONV_DIM)
    ob, s_new = _gdn(xb3, ba.reshape(bsz, t_len, LANES), zb.reshape(bsz, t_len, D_B), zero_conv, zero_state, 0,
                     conv_w, a_log, dt_bias, g_onorm, nb=1, tt=TOKEN_TILE, chunk=math.gcd(t_len, DELTA_CHUNK),
                     n_valid=TOKEN_TILE)
    y = _outproj(ot, za, ob.reshape(n, D_B), ga, gb, xf, gate, w_pa, w_pb, w_out, g_post, prompt=True,
                 tiles_per_seq=nb)
    heads = lambda a: jnp.transpose(a.reshape(bsz, H_A, DH_A, t_len), (0, 3, 1, 2))
    return y.reshape(bsz, t_len, D_MODEL), heads(k), heads(v), s_new, xb3[:, t_len - (CONV_W - 1):]


def _sample_layer(x, mod, lw, bias, cache_k, cache_v, page_table, state_delta, conv_pad, layer, dec_t):
    nseq = x.shape[0]
    n = nseq * SAMPLE_PAD
    g_pre, g_post, w_cat, conv_w, a_log, dt_bias, g_onorm, w_pa, w_pb, w_out = lw
    per_tok = jnp.repeat(mod, SAMPLE_PAD, axis=0)
    shift, scale, gate = (per_tok[:, j * D_MODEL:(j + 1) * D_MODEL] for j in range(3))
    xf = x.reshape(n, D_MODEL)
    q, k, v, za, xb, zb, ba, ga, gb = _inproj(xf, shift, scale, g_pre, w_cat, prompt=False)
    seq3 = lambda a: a.reshape(nseq, SAMPLE_PAD, a.shape[-1])
    oa = _moba_sample(seq3(q), seq3(k), seq3(v), cache_k, cache_v, page_table, bias, layer)
    xb3 = seq3(xb)
    ob, s_new = _gdn(xb3, seq3(ba), seq3(zb), conv_pad, state_delta, layer, conv_w, a_log, dt_bias, g_onorm,
                     nb=SUBLANES, tt=SAMPLE_PAD, chunk=SAMPLE_PAD, n_valid=dec_t)
    y = _outproj(oa.reshape(n, D_A), za, ob.reshape(n, D_B), ga, gb, xf, gate, w_pa, w_pb, w_out, g_post,
                 prompt=False)
    heads = lambda a: seq3(a)[:, :dec_t].reshape(nseq, dec_t, H_A, DH_A)
    return y.reshape(nseq, SAMPLE_PAD, D_MODEL), heads(k), heads(v), s_new, xb3[:, dec_t - (CONV_W - 1):dec_t]


def kernel(x_prompt, x_sample, c_prompt, c_sample, cache_k, cache_v, state_delta, state_conv, page_table, rel_bias, w_ada, b_ada, g_pre, g_post, w_in, conv_w, a_log, dt_bias, g_onorm, w_pa, w_pb, w_out):
    bsz, seq, _ = x_prompt.shape
    dec_b, dec_t, _ = x_sample.shape
    depth = w_in.shape[0]
    n_pages = page_table.shape[1]
    past_len = n_pages * PAGE_SIZE
    assert seq % MOBA_BLOCK == 0 and seq >= CONV_W - 1 and CONV_W - 1 <= dec_t <= SAMPLE_PAD
    assert dec_b % SUBLANES == 0

    bias_p = _bias_prompt(rel_bias)
    bias_s, far = _bias_sample(rel_bias, past_len, dec_t)
    far = far[:, 0]
    c_all = jnp.concatenate([c_prompt, c_sample], axis=0)
    c_all = jnp.pad(c_all, ((0, -c_all.shape[0] % (2 * SUBLANES)), (0, 0)))
    mod = _modulation(c_all, w_ada, b_ada)
    w_cat = _pack_w_in(w_in)
    w_pa_b, w_pb_b, w_out_b = w_pa.astype(bf16), w_pb.astype(bf16), w_out.astype(bf16)
    ck = jnp.transpose(cache_k, (0, 1, 3, 4, 2)).reshape(cache_k.shape[0], cache_k.shape[1], D_A, PAGE_SIZE)
    cv = jnp.transpose(cache_v, (0, 1, 3, 4, 2)).reshape(cache_v.shape[0], cache_v.shape[1], D_A, PAGE_SIZE)
    conv_pad = jnp.pad(state_conv, ((0, 0), (0, 0), (SUBLANES - (CONV_W - 1), 0), (0, 0)))
    zero_conv = jnp.zeros((1, bsz, SUBLANES, CONV_DIM), f32)
    zero_state = jnp.zeros((1, bsz, H_B, DK, DV), f32)

    hp = x_prompt
    hs = jnp.pad(x_sample, ((0, 0), (0, SAMPLE_PAD - dec_t), (0, 0)))
    outs = [[] for _ in range(8)]
    for l in range(depth):
        lw = (g_pre[l], g_post[l], w_cat[l], conv_w[l], a_log[l], dt_bias[l], g_onorm[l],
              w_pa_b[l], w_pb_b[l], w_out_b[l])
        hp, kp, vp, sp, cp = _prompt_layer(hp, mod[l, :bsz], lw, bias_p, far, zero_conv, zero_state)
        hs, ks, vs, ss, cs = _sample_layer(hs, mod[l, bsz:bsz + dec_b], lw, bias_s, ck, cv, page_table, state_delta,
                                           conv_pad, l, dec_t)
        for lst, val in zip(outs, (kp, vp, ks, vs, sp, ss, cp, cs)):
            lst.append(val)
    return (hp, hs[:, :dec_t]) + tuple(jnp.stack(o) for o in outs)
```

```python
import functools
import math

import jax
import jax.numpy as jnp
from jax import lax
from jax.experimental import pallas as pl
from jax.experimental.pallas import tpu as pltpu

f32 = jnp.float32
bf16 = jnp.bfloat16
HIGHEST = lax.Precision.HIGHEST

D_MODEL = 1024
D_A = D_MODEL // 2
DH_A = 64
H_A = D_A // DH_A
MOBA_BLOCK = 256
MOBA_TOPK = 3
N_BUCKETS = 32
MAX_DISTANCE = 1024
D_B = D_MODEL // 2
DK = 128
DV = 128
H_B = D_B // DV
CONV_W = 4
CONV_DIM = 2 * H_B * DK + H_B * DV
DELTA_CHUNK = 64
PAGE_SIZE = 128
EPS = 1e-6

LANES = 128
SUBLANES = 8
NEG = -1e30
TOKEN_TILE = MOBA_BLOCK
SAMPLE_PAD = SUBLANES
NEAR_TILES = MAX_DISTANCE // MOBA_BLOCK + 1
W_MAIN = 4 * D_A + CONV_DIM + H_B * DV
W_BA = W_MAIN
W_GATES = W_MAIN + LANES
W_TOTAL = W_GATES + 2 * D_MODEL
VMEM_LIMIT = 48 * 1024 * 1024


def _cparams(*sem):
    return pltpu.CompilerParams(dimension_semantics=sem, vmem_limit_bytes=VMEM_LIMIT)


def _dot(a, b):
    return jnp.dot(a.astype(bf16), b.astype(bf16), preferred_element_type=f32)


def _dot_nt(a, b):
    return lax.dot_general(a.astype(bf16), b.astype(bf16), (((1,), (1,)), ((), ())), preferred_element_type=f32)


def _dot_tn(a, b):
    return lax.dot_general(a.astype(bf16), b.astype(bf16), (((0,), (0,)), ((), ())), preferred_element_type=f32)


def _bmm(a, b):
    return lax.dot_general(a.astype(bf16), b.astype(bf16), (((2,), (1,)), ((0,), (0,))), preferred_element_type=f32)


def _bmm_nt(a, b):
    return lax.dot_general(a.astype(bf16), b.astype(bf16), (((2,), (2,)), ((0,), (0,))), preferred_element_type=f32)


def _bmm_tn(a, b):
    return lax.dot_general(a.astype(bf16), b.astype(bf16), (((1,), (1,)), ((0,), (0,))), preferred_element_type=f32)


def _silu(x):
    return x * jax.nn.sigmoid(x)


def _softplus(x):
    return jnp.maximum(x, 0.0) + jnp.log(1.0 + jnp.exp(-jnp.abs(x)))


def _div_pow2(x, n):
    return lax.shift_right_logical(x, int(math.log2(n)))


def _rel_bucket(dist):
    n = jnp.maximum(dist, 0)
    max_exact = N_BUCKETS // 2
    nf = jnp.maximum(n, 1).astype(f32)
    large = max_exact + (jnp.log(nf / max_exact) / math.log(MAX_DISTANCE / max_exact)
                         * (N_BUCKETS - max_exact)).astype(jnp.int32)
    large = jnp.minimum(large, N_BUCKETS - 1)
    return jnp.where(n < max_exact, n, large)


def _bias_prompt_kernel(rb_ref, o_ref):
    d = pl.program_id(0)
    h = pl.program_id(1)
    r = lax.broadcasted_iota(jnp.int32, (MOBA_BLOCK, MOBA_BLOCK), 0)
    c = lax.broadcasted_iota(jnp.int32, (MOBA_BLOCK, MOBA_BLOCK), 1)
    dist = d * MOBA_BLOCK + c - r
    bucket = _rel_bucket(dist)
    acc = jnp.zeros((MOBA_BLOCK, MOBA_BLOCK), f32)
    for b in range(N_BUCKETS):
        acc = jnp.where(bucket == b, rb_ref[b, h], acc)
    o_ref[...] = jnp.where(dist >= 0, acc, NEG)


def _bias_prompt(rel_bias):
    return pl.pallas_call(
        _bias_prompt_kernel,
        out_shape=jax.ShapeDtypeStruct((NEAR_TILES, H_A, MOBA_BLOCK, MOBA_BLOCK), f32),
        grid=(NEAR_TILES, H_A),
        in_specs=[pl.BlockSpec(memory_space=pltpu.SMEM)],
        out_specs=pl.BlockSpec((None, None, MOBA_BLOCK, MOBA_BLOCK), lambda d, h: (d, h, 0, 0)),
        compiler_params=_cparams("arbitrary", "arbitrary"),
        name="bias_prompt",
    )(rel_bias)


def _bias_sample_kernel(rb_ref, o_ref, far_ref, *, past_len, dec_t):
    rows = H_A * SAMPLE_PAD
    width = past_len + LANES
    row = lax.broadcasted_iota(jnp.int32, (rows, width), 0)
    lane = lax.broadcasted_iota(jnp.int32, (rows, width), 1)
    rowh = _div_pow2(row, SAMPLE_PAD)
    tq = jnp.minimum(row - rowh * SAMPLE_PAD, dec_t - 1)
    tn = lane - past_len
    is_new = lane >= past_len
    dist = jnp.where(is_new, tq - tn, past_len + tq - lane)
    valid = jnp.logical_or(jnp.logical_not(is_new), jnp.logical_and(tn <= tq, tn < dec_t))
    bucket = _rel_bucket(dist)
    far_bucket = _rel_bucket(jnp.full((H_A, LANES), MAX_DISTANCE, jnp.int32))
    far_row = lax.broadcasted_iota(jnp.int32, (H_A, LANES), 0)

    def body(b, carry):
        acc, far = carry
        for h in range(H_A):
            v = rb_ref[b, h]
            acc = jnp.where(jnp.logical_and(bucket == b, rowh == h), v, acc)
            far = jnp.where(jnp.logical_and(far_bucket == b, far_row == h), v, far)
        return acc, far

    acc, far = lax.fori_loop(0, N_BUCKETS, body, (jnp.zeros((rows, width), f32), jnp.zeros((H_A, LANES), f32)))
    o_ref[...] = jnp.where(valid, acc, NEG)
    far_ref[...] = far


def _bias_sample(rel_bias, past_len, dec_t):
    return pl.pallas_call(
        functools.partial(_bias_sample_kernel, past_len=past_len, dec_t=dec_t),
        out_shape=(jax.ShapeDtypeStruct((H_A * SAMPLE_PAD, past_len + LANES), f32),
                   jax.ShapeDtypeStruct((H_A, LANES), f32)),
        in_specs=[pl.BlockSpec(memory_space=pltpu.SMEM)],
        name="bias_sample",
    )(rel_bias)


def _mod_kernel(c_ref, w_ref, b_ref, o_ref):
    o_ref[...] = _dot(_silu(c_ref[...]), w_ref[...]) + b_ref[...]


def _modulation(c_all, w_ada, b_ada):
    depth = w_ada.shape[0]
    rows = c_all.shape[0]
    return pl.pallas_call(
        _mod_kernel,
        out_shape=jax.ShapeDtypeStruct((depth, rows, 3 * D_MODEL), f32),
        grid=(depth, 3),
        in_specs=[pl.BlockSpec((rows, D_MODEL), lambda l, j: (0, 0)),
                  pl.BlockSpec((None, D_MODEL, D_MODEL), lambda l, j: (l, 0, j)),
                  pl.BlockSpec((None, 1, D_MODEL), lambda l, j: (l, 0, j))],
        out_specs=pl.BlockSpec((None, rows, D_MODEL), lambda l, j: (l, 0, j)),
        compiler_params=_cparams("arbitrary", "arbitrary"),
        name="modulation",
    )(c_all, w_ada, b_ada.reshape(depth, 1, 3 * D_MODEL))


def _inproj_kernel(x_ref, shift_ref, scale_ref, gpre_ref, w_ref, *outs, prompt):
    x = x_ref[...]
    h = x * lax.rsqrt(jnp.mean(x * x, axis=-1, keepdims=True) + EPS) * gpre_ref[...]
    hb = (h * (1.0 + scale_ref[...]) + shift_ref[...]).astype(bf16)

    def proj(lo, hi):
        return jnp.dot(hb, w_ref[:, lo:hi], preferred_element_type=f32)

    q = proj(0, D_A)
    k = proj(D_A, 2 * D_A)
    v = proj(2 * D_A, 3 * D_A)
    if prompt:
        k_ref, v_ref, kb_ref, qt_ref, vt_ref, km_ref, za_ref, xb_ref, zb_ref, ba_ref, ga_ref, gb_ref = outs
        vt = v.T
        k_ref[...] = k.T
        v_ref[...] = vt
        kb_ref[...] = k.astype(bf16)
        qt_ref[...] = q.T
        vt_ref[...] = vt.astype(bf16)
        km_ref[...] = jnp.mean(k, axis=0, keepdims=True)
    else:
        q_ref, k_ref, v_ref, za_ref, xb_ref, zb_ref, ba_ref, ga_ref, gb_ref = outs
        q_ref[...] = q
        k_ref[...] = k
        v_ref[...] = v
    za_ref[...] = proj(3 * D_A, 4 * D_A)
    xb_ref[...] = proj(4 * D_A, 4 * D_A + CONV_DIM)
    zb_ref[...] = proj(4 * D_A + CONV_DIM, W_MAIN)
    ba_ref[...] = proj(W_BA, W_BA + LANES)
    ga_ref[...] = proj(W_GATES, W_GATES + D_MODEL)
    gb_ref[...] = proj(W_GATES + D_MODEL, W_TOTAL)


def _inproj(x, shift, scale, g_pre, w_cat, *, prompt, tiles_per_seq=None):
    n = x.shape[0]
    tm = TOKEN_TILE
    nt = n // tm
    row = lambda i: (i, 0)
    if prompt:
        mod_spec = pl.BlockSpec((None, 1, D_MODEL), lambda i: (i // tiles_per_seq, 0, 0))
    else:
        mod_spec = pl.BlockSpec((tm, D_MODEL), row)
    tok = lambda w, dt=f32: (jax.ShapeDtypeStruct((n, w), dt), pl.BlockSpec((tm, w), row))
    common = [tok(D_A), tok(CONV_DIM), tok(D_B), tok(LANES), tok(D_MODEL), tok(D_MODEL)]
    if prompt:
        seq_t = (jax.ShapeDtypeStruct((nt // tiles_per_seq, D_A, tiles_per_seq * tm), f32),
                 pl.BlockSpec((None, D_A, tm), lambda i: (i // tiles_per_seq, 0, i % tiles_per_seq)))
        outs = [seq_t, seq_t, tok(D_A, bf16),
                (jax.ShapeDtypeStruct((nt, D_A, tm), f32), pl.BlockSpec((None, D_A, tm), lambda i: (i, 0, 0))),
                (jax.ShapeDtypeStruct((nt, D_A, tm), bf16), pl.BlockSpec((None, D_A, tm), lambda i: (i, 0, 0))),
                (jax.ShapeDtypeStruct((nt, 1, D_A), f32), pl.BlockSpec((None, 1, D_A), lambda i: (i, 0, 0)))] + common
    else:
        outs = [tok(D_A), tok(D_A), tok(D_A)] + common
    return pl.pallas_call(
        functools.partial(_inproj_kernel, prompt=prompt),
        out_shape=tuple(o[0] for o in outs),
        grid=(nt,),
        in_specs=[pl.BlockSpec((tm, D_MODEL), row), mod_spec, mod_spec,
                  pl.BlockSpec((1, D_MODEL), lambda i: (0, 0)),
                  pl.BlockSpec((D_MODEL, W_TOTAL), lambda i: (0, 0), pipeline_mode=pl.Buffered(1))],
        out_specs=tuple(o[1] for o in outs),
        compiler_params=_cparams("arbitrary"),
        name="inproj_prompt" if prompt else "inproj_sample",
    )(x, shift, scale, g_pre.reshape(1, D_MODEL), w_cat)


def _moba_prompt_kernel(far_ref, qt_ref, kb_ref, vt_ref, km_ref, bias_ref, o_ref, sel_scr, qm_scr):
    hp = pl.program_id(1)
    i = pl.program_id(2)
    nb = km_ref.shape[0]
    qt = qt_ref[...]
    km = km_ref[...]
    blk = lax.broadcasted_iota(jnp.int32, (nb, MOBA_BLOCK), 0)
    feat = lax.broadcasted_iota(jnp.int32, (2 * DH_A, MOBA_BLOCK), 0)
    for hh in range(2):
        qm = jnp.where(jnp.logical_and(feat >= hh * DH_A, feat < (hh + 1) * DH_A), qt, 0.0)
        gate = jnp.dot(km, qm, precision=HIGHEST, preferred_element_type=f32)
        g = jnp.where(blk < i, gate, -jnp.inf)
        chosen = blk == i
        blkf = blk.astype(f32)
        for _ in range(MOBA_TOPK):
            m = jnp.max(g, axis=0, keepdims=True)
            first = jnp.min(jnp.where(g == m, blkf, float(nb)), axis=0, keepdims=True)
            hit = jnp.logical_and(blkf == first, m > -jnp.inf)
            chosen = jnp.logical_or(chosen, hit)
            g = jnp.where(hit, -jnp.inf, g)
        sel_scr[hh] = jnp.where(chosen, 0.0, NEG)
        qm_scr[hh] = (qm * DH_A ** -0.5).astype(bf16)

    def step(j, carry, bias_of):
        kj = kb_ref[j]
        vj = vt_ref[j]
        new = []
        for hh in range(2):
            m, l, acc = carry[hh]
            s = jnp.dot(kj, qm_scr[hh], preferred_element_type=f32)
            s = s + bias_of(j, hh) + sel_scr[hh, pl.ds(j, 1), :]
            m_new = jnp.maximum(m, jnp.max(s, axis=0, keepdims=True))
            alpha = jnp.exp(m - m_new)
            p = jnp.exp(s - m_new)
            l = alpha * l + jnp.sum(p, axis=0, keepdims=True)
            acc = alpha * acc + jnp.dot(vj[hh * DH_A:(hh + 1) * DH_A, :], p.astype(bf16),
                                        preferred_element_type=f32)
            new.append((m_new, l, acc))
        return tuple(new)

    init = tuple((jnp.full((1, MOBA_BLOCK), NEG, f32), jnp.zeros((1, MOBA_BLOCK), f32),
                  jnp.zeros((DH_A, MOBA_BLOCK), f32)) for _ in range(2))
    n_far = jnp.maximum(i - (NEAR_TILES - 1), 0)
    carry = lax.fori_loop(0, n_far, lambda j, c: step(j, c, lambda j, hh: far_ref[hp * 2 + hh]), init)
    carry = lax.fori_loop(n_far, i + 1, lambda j, c: step(j, c, lambda j, hh: bias_ref[i - j, hh]), carry)
    for hh in range(2):
        _, l, acc = carry[hh]
        o_ref[hh * DH_A:(hh + 1) * DH_A, :] = acc / l


def _moba_prompt(qt, kb, vt, kmean, bias, far):
    bsz, nb = kb.shape[0], kb.shape[1]
    hd = 2 * DH_A
    return pl.pallas_call(
        _moba_prompt_kernel,
        out_shape=jax.ShapeDtypeStruct((bsz * nb, D_A, MOBA_BLOCK), f32),
        grid=(bsz, H_A // 2, nb),
        in_specs=[pl.BlockSpec(memory_space=pltpu.SMEM),
                  pl.BlockSpec((None, hd, MOBA_BLOCK), lambda b, hp, i: (b * nb + i, hp, 0)),
                  pl.BlockSpec((None, nb, MOBA_BLOCK, hd), lambda b, hp, i: (b, 0, 0, hp)),
                  pl.BlockSpec((None, nb, hd, MOBA_BLOCK), lambda b, hp, i: (b, 0, hp, 0)),
                  pl.BlockSpec((None, nb, hd), lambda b, hp, i: (b, 0, hp)),
                  pl.BlockSpec((NEAR_TILES, 2, MOBA_BLOCK, MOBA_BLOCK), lambda b, hp, i: (0, hp, 0, 0))],
        out_specs=pl.BlockSpec((None, hd, MOBA_BLOCK), lambda b, hp, i: (b * nb + i, hp, 0)),
        scratch_shapes=[pltpu.VMEM((2, nb, MOBA_BLOCK), f32), pltpu.VMEM((2, hd, MOBA_BLOCK), bf16)],
        compiler_params=_cparams("arbitrary", "arbitrary", "arbitrary"),
        name="moba_prompt",
    )(far, qt, kb, vt, kmean, bias)


def _moba_sample_kernel(pt_ref, q_ref, kn_ref, vn_ref, bias_ref, *refs, n_pages):
    del pt_ref
    kp = refs[:n_pages]
    vp = refs[n_pages:2 * n_pages]
    o_ref = refs[2 * n_pages]
    rows = H_A * SAMPLE_PAD
    ppb = MOBA_BLOCK // PAGE_SIZE
    n_blocks = n_pages // ppb
    rowh = _div_pow2(lax.broadcasted_iota(jnp.int32, (rows, D_A), 0), SAMPLE_PAD)
    laneh = _div_pow2(lax.broadcasted_iota(jnp.int32, (rows, D_A), 1), DH_A)
    own = rowh == laneh
    qe = jnp.where(own, jnp.concatenate([q_ref[...]] * H_A, axis=0), 0.0)
    qeb = (qe * DH_A ** -0.5).astype(bf16)

    lane_f = lax.broadcasted_iota(jnp.int32, (D_A, LANES), 1)
    kmean = jnp.zeros((D_A, LANES), f32)
    for j in range(n_blocks):
        ksum = kp[j * ppb][...]
        for p in range(1, ppb):
            ksum = ksum + kp[j * ppb + p][...]
        kmean = jnp.where(lane_f == j, jnp.sum(ksum, axis=1, keepdims=True) / MOBA_BLOCK, kmean)
    gate = jnp.dot(qe, kmean, precision=HIGHEST, preferred_element_type=f32)
    lane_r = lax.broadcasted_iota(jnp.int32, (rows, LANES), 1)
    lane_rf = lane_r.astype(f32)
    g = jnp.where(lane_r < n_blocks, gate, -jnp.inf)
    chosen = jnp.zeros((rows, LANES), jnp.bool_)
    for _ in range(MOBA_TOPK):
        m = jnp.max(g, axis=1, keepdims=True)
        first = jnp.min(jnp.where(g == m, lane_rf, float(LANES)), axis=1, keepdims=True)
        hit = lane_rf == first
        chosen = jnp.logical_or(chosen, hit)
        g = jnp.where(hit, -jnp.inf, g)
    sel = jnp.where(chosen, 0.0, NEG)

    pad_rows = LANES - SAMPLE_PAD
    kn = jnp.concatenate([kn_ref[...], jnp.zeros((pad_rows, D_A), f32)], axis=0)
    vn = jnp.concatenate([vn_ref[...], jnp.zeros((pad_rows, D_A), f32)], axis=0)
    s = []
    for p in range(n_pages):
        sp = _dot(qeb, kp[p][...]) + bias_ref[:, p * PAGE_SIZE:(p + 1) * PAGE_SIZE]
        s.append(sp + sel[:, p // ppb:p // ppb + 1])
    s.append(_dot_nt(qeb, kn) + bias_ref[:, n_pages * PAGE_SIZE:n_pages * PAGE_SIZE + LANES])
    m = s[0]
    for sp in s[1:]:
        m = jnp.maximum(m, sp)
    m = jnp.max(m, axis=-1, keepdims=True)
    l = jnp.zeros((rows, 1), f32)
    acc = jnp.zeros((rows, D_A), f32)
    for p in range(n_pages + 1):
        e = jnp.exp(s[p] - m)
        l = l + jnp.sum(e, axis=-1, keepdims=True)
        acc = acc + (_dot_nt(e, vp[p][...]) if p < n_pages else _dot(e, vn))
    o = jnp.where(own, acc / l, 0.0)
    out = o[0:SAMPLE_PAD]
    for h in range(1, H_A):
        out = out + o[h * SAMPLE_PAD:(h + 1) * SAMPLE_PAD]
    o_ref[...] = out


def _moba_sample(q, kn, vn, cache_k, cache_v, page_table, bias, layer):
    nseq, n_pages = page_table.shape
    assert (n_pages * PAGE_SIZE) % MOBA_BLOCK == 0 and (n_pages * PAGE_SIZE) // MOBA_BLOCK >= MOBA_TOPK
    seq = pl.BlockSpec((None, SAMPLE_PAD, D_A), lambda b, pt: (b, 0, 0))
    page = lambda p: pl.BlockSpec((None, None, D_A, PAGE_SIZE), lambda b, pt: (layer, pt[b, p], 0, 0))
    grid_spec = pltpu.PrefetchScalarGridSpec(
        num_scalar_prefetch=1,
        grid=(nseq,),
        in_specs=[seq, seq, seq, pl.BlockSpec(bias.shape, lambda b, pt: (0, 0))]
        + [page(p) for p in range(n_pages)] * 2,
        out_specs=seq,
    )
    return pl.pallas_call(
        functools.partial(_moba_sample_kernel, n_pages=n_pages),
        out_shape=jax.ShapeDtypeStruct((nseq, SAMPLE_PAD, D_A), f32),
        grid_spec=grid_spec,
        compiler_params=_cparams("arbitrary"),
        name="moba_sample",
    )(page_table, q, kn, vn, bias, *([cache_k] * n_pages), *([cache_v] * n_pages))


def _gdn_kernel(x_ref, ba_ref, z_ref, conv0_ref, s0_ref, cw_ref, avec_ref, dtvec_ref, gon_ref,
                o_ref, s_ref, xs_scr, *, chunk, n_valid):
    t = pl.program_id(1)
    nb, tt = x_ref.shape[0], x_ref.shape[1]

    @pl.when(t == 0)
    def _():
        s_ref[...] = s0_ref[...]
        xs_scr[:, 0:SUBLANES, :] = conv0_ref[...]

    xs_scr[:, SUBLANES:SUBLANES + tt, :] = x_ref[...]

    ci = lax.broadcasted_iota(jnp.int32, (chunk, chunk), 0)
    cj = lax.broadcasted_iota(jnp.int32, (chunk, chunk), 1)
    causal = ci >= cj
    strict = ci > cj
    tri = causal.astype(f32)
    eye = (ci == cj).astype(f32)
    cw = cw_ref[...]
    neg_a = -jnp.exp(avec_ref[...])
    merges = []
    for b in range(int(math.log2(chunk))):
        same_pair = lax.shift_right_logical(ci, b + 1) == lax.shift_right_logical(cj, b + 1)
        row_hi = jnp.bitwise_and(lax.shift_right_logical(ci, b), 1) == 1
        col_lo = jnp.bitwise_and(lax.shift_right_logical(cj, b), 1) == 0
        merges.append(jnp.logical_and(same_pair, jnp.logical_and(row_hi, col_lo)))

    nc = tt // chunk
    qs, ks, vs, zs, bcols, gcols = [], [], [], [], [], []
    base = SUBLANES - (CONV_W - 1)
    for n in range(nb):
        u = xs_scr[n, base:base + tt, :] * cw[0:1]
        for j in range(1, CONV_W):
            u = u + xs_scr[n, base + j:base + j + tt, :] * cw[j:j + 1]
        u = _silu(u)
        ba = ba_ref[n]
        beta = jax.nn.sigmoid(ba)
        g = neg_a * _softplus(ba + dtvec_ref[...])
        if n_valid < tt:
            keep = lax.broadcasted_iota(jnp.int32, (tt, LANES), 0) < n_valid
            beta = jnp.where(keep, beta, 0.0)
            g = jnp.where(keep, g, 0.0)
        for c in range(nc):
            rows = slice(c * chunk, (c + 1) * chunk)
            gcum = jnp.dot(tri, g[rows], precision=HIGHEST, preferred_element_type=f32)
            for h in range(H_B):
                qs.append(u[rows, h * DK:(h + 1) * DK])
                ks.append(u[rows, H_B * DK + h * DK:H_B * DK + (h + 1) * DK])
                vs.append(u[rows, 2 * H_B * DK + h * DV:2 * H_B * DK + (h + 1) * DV])
                zs.append(z_ref[n, rows, h * DV:(h + 1) * DV])
                bcols.append(beta[rows, h:h + 1])
                gcols.append(gcum[:, H_B + h:H_B + h + 1])
    q, k, v, z = jnp.stack(qs), jnp.stack(ks), jnp.stack(vs), jnp.stack(zs)
    bcol, gcol = jnp.stack(bcols), jnp.stack(gcols)
    q = q * lax.rsqrt(jnp.sum(q * q, axis=-1, keepdims=True) + EPS) * DK ** -0.5
    k = k * lax.rsqrt(jnp.sum(k * k, axis=-1, keepdims=True) + EPS)
    grow = jnp.sum(eye[None] * gcol, axis=1, keepdims=True)
    decay = jnp.exp(jnp.where(causal[None], gcol - grow, NEG))
    kb = k * bcol
    lower = jnp.where(strict[None], _bmm_nt(kb, k) * decay, 0.0)
    inv = eye[None] - jnp.where(merges[0][None], lower, 0.0)
    for merge in merges[1:]:
        inv = inv - _bmm(_bmm(inv, jnp.where(merge[None], lower, 0.0)), inv)
    sol = _bmm(inv, jnp.concatenate([v * bcol, kb * jnp.exp(gcol)], axis=2))
    uu, ww = sol[:, :, :DV], sol[:, :, DV:]
    qk = jnp.where(causal[None], _bmm_nt(q, k) * decay, 0.0)
    qe = q * jnp.exp(gcol)
    g_last = gcol[:, chunk - 1:chunk, :]
    ke = k * jnp.exp(g_last - gcol)
    carry = jnp.exp(g_last)

    def at_chunk(a, c):
        return a.reshape((nb, nc, H_B) + a.shape[1:])[:, c].reshape((nb * H_B,) + a.shape[1:])

    state = s_ref[...].reshape(nb * H_B, DK, DV)
    for c in range(nc):
        v_new = at_chunk(uu, c) - _bmm(at_chunk(ww, c), state)
        o = _bmm(at_chunk(qe, c), state) + _bmm(at_chunk(qk, c), v_new)
        state = state * at_chunk(carry, c) + _bmm_tn(at_chunk(ke, c), v_new)
        o = o * lax.rsqrt(jnp.mean(o * o, axis=-1, keepdims=True) + EPS) * gon_ref[...]
        o = o * _silu(at_chunk(z, c))
        for n in range(nb):
            for h in range(H_B):
                o_ref[n, c * chunk:(c + 1) * chunk, h * DV:(h + 1) * DV] = o[n * H_B + h]
    s_ref[...] = state.reshape(nb, H_B, DK, DV)

    xs_scr[:, 0:SUBLANES, :] = xs_scr[:, tt:tt + SUBLANES, :]


def _gdn(xb, ba, zb, conv0, s0, layer, conv_w, a_log, dt_bias, g_onorm, *, nb, tt, chunk, n_valid):
    nseq, t_len = xb.shape[0], xb.shape[1]
    lane_vec = lambda v: jnp.zeros((1, LANES), f32).at[0, H_B:2 * H_B].set(v)
    tok = lambda w: pl.BlockSpec((nb, tt, w), lambda g, t: (g, t, 0))
    const = lambda shape: pl.BlockSpec(shape, lambda g, t: (0,) * len(shape))
    return pl.pallas_call(
        functools.partial(_gdn_kernel, chunk=chunk, n_valid=n_valid),
        out_shape=(jax.ShapeDtypeStruct((nseq, t_len, D_B), f32),
                   jax.ShapeDtypeStruct((nseq, H_B, DK, DV), f32)),
        grid=(nseq // nb, t_len // tt),
        in_specs=[tok(CONV_DIM), tok(LANES), tok(D_B),
                  pl.BlockSpec((None, nb, SUBLANES, CONV_DIM), lambda g, t: (layer, g, 0, 0)),
                  pl.BlockSpec((None, nb, H_B, DK, DV), lambda g, t: (layer, g, 0, 0, 0)),
                  const((CONV_W, CONV_DIM)), const((1, LANES)), const((1, LANES)), const((1, DV))],
        out_specs=(tok(D_B), pl.BlockSpec((nb, H_B, DK, DV), lambda g, t: (g, 0, 0, 0))),
        scratch_shapes=[pltpu.VMEM((nb, tt + SUBLANES, CONV_DIM), f32)],
        compiler_params=_cparams("arbitrary", "arbitrary"),
        name="gdn_%d" % tt,
    )(xb, ba, zb, conv0, s0, conv_w, lane_vec(a_log), lane_vec(dt_bias), g_onorm.reshape(1, DV))


def _outproj_kernel(oa_ref, za_ref, ob_ref, ga_ref, gb_ref, x_ref, gate_ref, wpa_ref, wpb_ref, wout_ref,
                    gpost_ref, y_ref, *, prompt):
    oa = oa_ref[...].T if prompt else oa_ref[...]
    oa = oa * _silu(za_ref[...])
    merged = (jax.nn.sigmoid(ga_ref[...]) * _dot(oa, wpa_ref[...])
              + jax.nn.sigmoid(gb_ref[...]) * _dot(ob_ref[...], wpb_ref[...]))
    y = _dot(merged, wout_ref[...])
    y = y * lax.rsqrt(jnp.mean(y * y, axis=-1, keepdims=True) + EPS) * gpost_ref[...]
    y_ref[...] = x_ref[...] + gate_ref[...] * y


def _outproj(oa, za, ob, ga, gb, x, gate, w_pa, w_pb, w_out, g_post, *, prompt, tiles_per_seq=None):
    n = x.shape[0]
    tm = TOKEN_TILE
    row = lambda i: (i, 0)
    tok = lambda w: pl.BlockSpec((tm, w), row)
    const = lambda shape: pl.BlockSpec(shape, lambda i: (0, 0))
    if prompt:
        oa_spec = pl.BlockSpec((None, D_A, tm), lambda i: (i, 0, 0))
        gate_spec = pl.BlockSpec((None, 1, D_MODEL), lambda i: (i // tiles_per_seq, 0, 0))
    else:
        oa_spec = tok(D_A)
        gate_spec = tok(D_MODEL)
    return pl.pallas_call(
        functools.partial(_outproj_kernel, prompt=prompt),
        out_shape=jax.ShapeDtypeStruct((n, D_MODEL), f32),
        grid=(n // tm,),
        in_specs=[oa_spec, tok(D_A), tok(D_B), tok(D_MODEL), tok(D_MODEL), tok(D_MODEL), gate_spec,
                  const((D_A, D_MODEL)), const((D_B, D_MODEL)), const((D_MODEL, D_MODEL)), const((1, D_MODEL))],
        out_specs=tok(D_MODEL),
        compiler_params=_cparams("arbitrary"),
        name="outproj_prompt" if prompt else "outproj_sample",
    )(oa, za, ob, ga, gb, x, gate, w_pa, w_pb, w_out, g_post.reshape(1, D_MODEL))


def _pack_w_in(w_in):
    depth = w_in.shape[0]
    n_ba = 2 * H_B
    ba = jnp.zeros((depth, D_MODEL, LANES), w_in.dtype).at[:, :, :n_ba].set(w_in[:, :, W_MAIN:W_MAIN + n_ba])
    return jnp.concatenate([w_in[:, :, :W_MAIN], ba, w_in[:, :, W_MAIN + n_ba:]], axis=-1).astype(bf16)


def _prompt_layer(x, mod, lw, bias, far, zero_conv, zero_state):
    bsz, t_len, _ = x.shape
    n = bsz * t_len
    nb = t_len // MOBA_BLOCK
    g_pre, g_post, w_cat, conv_w, a_log, dt_bias, g_onorm, w_pa, w_pb, w_out = lw
    shift, scale, gate = (mod[:, None, j * D_MODEL:(j + 1) * D_MODEL] for j in range(3))
    xf = x.reshape(n, D_MODEL)
    k, v, kb, qt, vt, km, za, xb, zb, ba, ga, gb = _inproj(xf, shift, scale, g_pre, w_cat, prompt=True,
                                                             tiles_per_seq=nb)
    ot = _moba_prompt(qt, kb.reshape(bsz, nb, MOBA_BLOCK, D_A), vt.reshape(bsz, nb, D_A, MOBA_BLOCK),
                      km.reshape(bsz, nb, D_A), bias, far)
    xb3 = xb.reshape(bsz, t_len, CONV_DIM)
    ob, s_new = _gdn(xb3, ba.reshape(bsz, t_len, LANES), zb.reshape(bsz, t_len, D_B), zero_conv, zero_state, 0,
                     conv_w, a_log, dt_bias, g_onorm, nb=1, tt=TOKEN_TILE, chunk=math.gcd(t_len, DELTA_CHUNK),
                     n_valid=TOKEN_TILE)
    y = _outproj(ot, za, ob.reshape(n, D_B), ga, gb, xf, gate, w_pa, w_pb, w_out, g_post, prompt=True,
                 tiles_per_seq=nb)
    heads = lambda a: jnp.transpose(a.reshape(bsz, H_A, DH_A, t_len), (0, 3, 1, 2))
    return y.reshape(bsz, t_len, D_MODEL), heads(k), heads(v), s_new, xb3[:, t_len - (CONV_W - 1):]


def _sample_layer(x, mod, lw, bias, cache_k, cache_v, page_table, state_delta, conv_pad, layer, dec_t):
    nseq = x.shape[0]
    n = nseq * SAMPLE_PAD
    g_pre, g_post, w_cat, conv_w, a_log, dt_bias, g_onorm, w_pa, w_pb, w_out = lw
    per_tok = jnp.repeat(mod, SAMPLE_PAD, axis=0)
    shift, scale, gate = (per_tok[:, j * D_MODEL:(j + 1) * D_MODEL] for j in range(3))
    xf = x.reshape(n, D_MODEL)
    q, k, v, za, xb, zb, ba, ga, gb = _inproj(xf, shift, scale, g_pre, w_cat, prompt=False)
    seq3 = lambda a: a.reshape(nseq, SAMPLE_PAD, a.shape[-1])
    oa = _moba_sample(seq3(q), seq3(k), seq3(v), cache_k, cache_v, page_table, bias, layer)
    xb3 = seq3(xb)
    ob, s_new = _gdn(xb3, seq3(ba), seq3(zb), conv_pad, state_delta, layer, conv_w, a_log, dt_bias, g_onorm,
                     nb=SUBLANES, tt=SAMPLE_PAD, chunk=SAMPLE_PAD, n_valid=dec_t)
    y = _outproj(oa.reshape(n, D_A), za, ob.reshape(n, D_B), ga, gb, xf, gate, w_pa, w_pb, w_out, g_post,
                 prompt=False)
    heads = lambda a: seq3(a)[:, :dec_t].reshape(nseq, dec_t, H_A, DH_A)
    return y.reshape(nseq, SAMPLE_PAD, D_MODEL), heads(k), heads(v), s_new, xb3[:, dec_t - (CONV_W - 1):dec_t]


def kernel(x_prompt, x_sample, c_prompt, c_sample, cache_k, cache_v, state_delta, state_conv, page_table, rel_bias, w_ada, b_ada, g_pre, g_post, w_in, conv_w, a_log, dt_bias, g_onorm, w_pa, w_pb, w_out):
    bsz, seq, _ = x_prompt.shape
    dec_b, dec_t, _ = x_sample.shape
    depth = w_in.shape[0]
    n_pages = page_table.shape[1]
    past_len = n_pages * PAGE_SIZE
    assert seq % MOBA_BLOCK == 0 and seq >= CONV_W - 1 and CONV_W - 1 <= dec_t <= SAMPLE_PAD
    assert dec_b % SUBLANES == 0

    bias_p = _bias_prompt(rel_bias)
    bias_s, far = _bias_sample(rel_bias, past_len, dec_t)
    far = far[:, 0]
    c_all = jnp.concatenate([c_prompt, c_sample], axis=0)
    c_all = jnp.pad(c_all, ((0, -c_all.shape[0] % (2 * SUBLANES)), (0, 0)))
    mod = _modulation(c_all, w_ada, b_ada)
    w_cat = _pack_w_in(w_in)
    w_pa_b, w_pb_b, w_out_b = w_pa.astype(bf16), w_pb.astype(bf16), w_out.astype(bf16)
    ck = jnp.transpose(cache_k, (0, 1, 3, 4, 2)).reshape(cache_k.shape[0], cache_k.shape[1], D_A, PAGE_SIZE)
    cv = jnp.transpose(cache_v, (0, 1, 3, 4, 2)).reshape(cache_v.shape[0], cache_v.shape[1], D_A, PAGE_SIZE)
    conv_pad = jnp.pad(state_conv, ((0, 0), (0, 0), (SUBLANES - (CONV_W - 1), 0), (0, 0)))
    zero_conv = jnp.zeros((1, bsz, SUBLANES, CONV_DIM), f32)
    zero_state = jnp.zeros((1, bsz, H_B, DK, DV), f32)

    hp = x_prompt
    hs = jnp.pad(x_sample, ((0, 0), (0, SAMPLE_PAD - dec_t), (0, 0)))
    outs = [[] for _ in range(8)]
    for l in range(depth):
        lw = (g_pre[l], g_post[l], w_cat[l], conv_w[l], a_log[l], dt_bias[l], g_onorm[l],
              w_pa_b[l], w_pb_b[l], w_out_b[l])
        hp, kp, vp, sp, cp = _prompt_layer(hp, mod[l, :bsz], lw, bias_p, far, zero_conv, zero_state)
        hs, ks, vs, ss, cs = _sample_layer(hs, mod[l, bsz:bsz + dec_b], lw, bias_s, ck, cv, page_table, state_delta,
                                           conv_pad, l, dec_t)
        for lst, val in zip(outs, (kp, vp, ks, vs, sp, ss, cp, cs)):
            lst.append(val)
    return (hp, hs[:, :dec_t]) + tuple(jnp.stack(o) for o in outs)
```

```python
import functools
import math

import jax
import jax.numpy as jnp
from jax import lax
from jax.experimental import pallas as pl
from jax.experimental.pallas import tpu as pltpu

f32 = jnp.float32
bf16 = jnp.bfloat16
HIGHEST = lax.Precision.HIGHEST

D_MODEL = 1024
D_A = D_MODEL // 2
DH_A = 64
H_A = D_A // DH_A
MOBA_BLOCK = 256
MOBA_TOPK = 3
N_BUCKETS = 32
MAX_DISTANCE = 1024
D_B = D_MODEL // 2
DK = 128
DV = 128
H_B = D_B // DV
CONV_W = 4
CONV_DIM = 2 * H_B * DK + H_B * DV
DELTA_CHUNK = 64
PAGE_SIZE = 128
EPS = 1e-6

LANES = 128
SUBLANES = 8
NEG = -1e30
TOKEN_TILE = MOBA_BLOCK
SAMPLE_PAD = SUBLANES
NEAR_TILES = MAX_DISTANCE // MOBA_BLOCK + 1
MOBA_HEAD_GROUP = 8
W_MAIN = 4 * D_A + CONV_DIM + H_B * DV
W_BA = W_MAIN
W_GATES = W_MAIN + LANES
W_TOTAL = W_GATES + 2 * D_MODEL
VMEM_LIMIT = 48 * 1024 * 1024


def _cparams(*sem):
    return pltpu.CompilerParams(dimension_semantics=sem, vmem_limit_bytes=VMEM_LIMIT)


def _dot(a, b):
    return jnp.dot(a.astype(bf16), b.astype(bf16), preferred_element_type=f32)


def _dot_nt(a, b):
    return lax.dot_general(a.astype(bf16), b.astype(bf16), (((1,), (1,)), ((), ())), preferred_element_type=f32)


def _dot_tn(a, b):
    return lax.dot_general(a.astype(bf16), b.astype(bf16), (((0,), (0,)), ((), ())), preferred_element_type=f32)


def _bmm(a, b):
    return lax.dot_general(a.astype(bf16), b.astype(bf16), (((2,), (1,)), ((0,), (0,))), preferred_element_type=f32)


def _bmm_nt(a, b):
    return lax.dot_general(a.astype(bf16), b.astype(bf16), (((2,), (2,)), ((0,), (0,))), preferred_element_type=f32)


def _bmm_tn(a, b):
    return lax.dot_general(a.astype(bf16), b.astype(bf16), (((1,), (1,)), ((0,), (0,))), preferred_element_type=f32)


def _silu(x):
    return x * jax.nn.sigmoid(x)


def _softplus(x):
    return jnp.maximum(x, 0.0) + jnp.log(1.0 + jnp.exp(-jnp.abs(x)))


def _div_pow2(x, n):
    return lax.shift_right_logical(x, int(math.log2(n)))


def _rel_bucket(dist):
    n = jnp.maximum(dist, 0)
    max_exact = N_BUCKETS // 2
    nf = jnp.maximum(n, 1).astype(f32)
    large = max_exact + (jnp.log(nf / max_exact) / math.log(MAX_DISTANCE / max_exact)
                         * (N_BUCKETS - max_exact)).astype(jnp.int32)
    large = jnp.minimum(large, N_BUCKETS - 1)
    return jnp.where(n < max_exact, n, large)


def _bias_prompt_kernel(rb_ref, o_ref):
    d = pl.program_id(0)
    h = pl.program_id(1)
    r = lax.broadcasted_iota(jnp.int32, (MOBA_BLOCK, MOBA_BLOCK), 0)
    c = lax.broadcasted_iota(jnp.int32, (MOBA_BLOCK, MOBA_BLOCK), 1)
    dist = d * MOBA_BLOCK + c - r
    bucket = _rel_bucket(dist)
    acc = jnp.zeros((MOBA_BLOCK, MOBA_BLOCK), f32)
    for b in range(N_BUCKETS):
        acc = jnp.where(bucket == b, rb_ref[b, h], acc)
    o_ref[...] = jnp.where(dist >= 0, acc, NEG)


def _bias_prompt(rel_bias):
    return pl.pallas_call(
        _bias_prompt_kernel,
        out_shape=jax.ShapeDtypeStruct((NEAR_TILES, H_A, MOBA_BLOCK, MOBA_BLOCK), f32),
        grid=(NEAR_TILES, H_A),
        in_specs=[pl.BlockSpec(memory_space=pltpu.SMEM)],
        out_specs=pl.BlockSpec((None, None, MOBA_BLOCK, MOBA_BLOCK), lambda d, h: (d, h, 0, 0)),
        compiler_params=_cparams("arbitrary", "arbitrary"),
        name="bias_prompt",
    )(rel_bias)


def _bias_sample_kernel(rb_ref, o_ref, far_ref, *, past_len, dec_t):
    rows = H_A * SAMPLE_PAD
    width = past_len + LANES
    row = lax.broadcasted_iota(jnp.int32, (rows, width), 0)
    lane = lax.broadcasted_iota(jnp.int32, (rows, width), 1)
    rowh = _div_pow2(row, SAMPLE_PAD)
    tq = jnp.minimum(row - rowh * SAMPLE_PAD, dec_t - 1)
    tn = lane - past_len
    is_new = lane >= past_len
    dist = jnp.where(is_new, tq - tn, past_len + tq - lane)
    valid = jnp.logical_or(jnp.logical_not(is_new), jnp.logical_and(tn <= tq, tn < dec_t))
    bucket = _rel_bucket(dist)
    far_bucket = _rel_bucket(jnp.full((H_A, LANES), MAX_DISTANCE, jnp.int32))
    far_row = lax.broadcasted_iota(jnp.int32, (H_A, LANES), 0)

    def body(b, carry):
        acc, far = carry
        for h in range(H_A):
            v = rb_ref[b, h]
            acc = jnp.where(jnp.logical_and(bucket == b, rowh == h), v, acc)
            far = jnp.where(jnp.logical_and(far_bucket == b, far_row == h), v, far)
        return acc, far

    acc, far = lax.fori_loop(0, N_BUCKETS, body, (jnp.zeros((rows, width), f32), jnp.zeros((H_A, LANES), f32)))
    o_ref[...] = jnp.where(valid, acc, NEG)
    far_ref[...] = far


def _bias_sample(rel_bias, past_len, dec_t):
    return pl.pallas_call(
        functools.partial(_bias_sample_kernel, past_len=past_len, dec_t=dec_t),
        out_shape=(jax.ShapeDtypeStruct((H_A * SAMPLE_PAD, past_len + LANES), f32),
                   jax.ShapeDtypeStruct((H_A, LANES), f32)),
        in_specs=[pl.BlockSpec(memory_space=pltpu.SMEM)],
        name="bias_sample",
    )(rel_bias)


def _mod_kernel(c_ref, w_ref, b_ref, o_ref):
    o_ref[...] = _dot(_silu(c_ref[...]), w_ref[...]) + b_ref[...]


def _modulation(c_all, w_ada, b_ada):
    depth = w_ada.shape[0]
    rows = c_all.shape[0]
    return pl.pallas_call(
        _mod_kernel,
        out_shape=jax.ShapeDtypeStruct((depth, rows, 3 * D_MODEL), f32),
        grid=(depth, 3),
        in_specs=[pl.BlockSpec((rows, D_MODEL), lambda l, j: (0, 0)),
                  pl.BlockSpec((None, D_MODEL, D_MODEL), lambda l, j: (l, 0, j)),
                  pl.BlockSpec((None, 1, D_MODEL), lambda l, j: (l, 0, j))],
        out_specs=pl.BlockSpec((None, rows, D_MODEL), lambda l, j: (l, 0, j)),
        compiler_params=_cparams("arbitrary", "arbitrary"),
        name="modulation",
    )(c_all, w_ada, b_ada.reshape(depth, 1, 3 * D_MODEL))


def _inproj_kernel(x_ref, shift_ref, scale_ref, gpre_ref, w_ref, *outs, prompt):
    x = x_ref[...]
    h = x * lax.rsqrt(jnp.mean(x * x, axis=-1, keepdims=True) + EPS) * gpre_ref[...]
    hb = (h * (1.0 + scale_ref[...]) + shift_ref[...]).astype(bf16)

    def proj(lo, hi):
        return jnp.dot(hb, w_ref[:, lo:hi], preferred_element_type=f32)

    q = proj(0, D_A)
    k = proj(D_A, 2 * D_A)
    v = proj(2 * D_A, 3 * D_A)
    if prompt:
        k_ref, v_ref, kb_ref, qt_ref, vt_ref, km_ref, za_ref, xb_ref, zb_ref, ba_ref, ga_ref, gb_ref = outs
        vt = v.T
        k_ref[...] = k.T
        v_ref[...] = vt
        kb_ref[...] = k.astype(bf16)
        qt_ref[...] = q.T
        vt_ref[...] = vt.astype(bf16)
        km_ref[...] = jnp.mean(k, axis=0, keepdims=True)
    else:
        q_ref, k_ref, v_ref, za_ref, xb_ref, zb_ref, ba_ref, ga_ref, gb_ref = outs
        q_ref[...] = q
        k_ref[...] = k
        v_ref[...] = v
    za_ref[...] = proj(3 * D_A, 4 * D_A)
    xb_ref[...] = proj(4 * D_A, 4 * D_A + CONV_DIM)
    zb_ref[...] = proj(4 * D_A + CONV_DIM, W_MAIN)
    ba_ref[...] = proj(W_BA, W_BA + LANES)
    ga_ref[...] = proj(W_GATES, W_GATES + D_MODEL)
    gb_ref[...] = proj(W_GATES + D_MODEL, W_TOTAL)


def _inproj(x, shift, scale, g_pre, w_cat, *, prompt, tiles_per_seq=None):
    n = x.shape[0]
    tm = TOKEN_TILE
    nt = n // tm
    row = lambda i: (i, 0)
    if prompt:
        mod_spec = pl.BlockSpec((None, 1, D_MODEL), lambda i: (i // tiles_per_seq, 0, 0))
    else:
        mod_spec = pl.BlockSpec((tm, D_MODEL), row)
    tok = lambda w, dt=f32: (jax.ShapeDtypeStruct((n, w), dt), pl.BlockSpec((tm, w), row))
    common = [tok(D_A), tok(CONV_DIM), tok(D_B), tok(LANES), tok(D_MODEL), tok(D_MODEL)]
    if prompt:
        seq_t = (jax.ShapeDtypeStruct((nt // tiles_per_seq, D_A, tiles_per_seq * tm), f32),
                 pl.BlockSpec((None, D_A, tm), lambda i: (i // tiles_per_seq, 0, i % tiles_per_seq)))
        outs = [seq_t, seq_t, tok(D_A, bf16),
                (jax.ShapeDtypeStruct((nt, D_A, tm), f32), pl.BlockSpec((None, D_A, tm), lambda i: (i, 0, 0))),
                (jax.ShapeDtypeStruct((nt, D_A, tm), bf16), pl.BlockSpec((None, D_A, tm), lambda i: (i, 0, 0))),
                (jax.ShapeDtypeStruct((nt, 1, D_A), f32), pl.BlockSpec((None, 1, D_A), lambda i: (i, 0, 0)))] + common
    else:
        outs = [tok(D_A), tok(D_A), tok(D_A)] + common
    return pl.pallas_call(
        functools.partial(_inproj_kernel, prompt=prompt),
        out_shape=tuple(o[0] for o in outs),
        grid=(nt,),
        in_specs=[pl.BlockSpec((tm, D_MODEL), row), mod_spec, mod_spec,
                  pl.BlockSpec((1, D_MODEL), lambda i: (0, 0)),
                  pl.BlockSpec((D_MODEL, W_TOTAL), lambda i: (0, 0), pipeline_mode=pl.Buffered(1))],
        out_specs=tuple(o[1] for o in outs),
        compiler_params=_cparams("arbitrary"),
        name="inproj_prompt" if prompt else "inproj_sample",
    )(x, shift, scale, g_pre.reshape(1, D_MODEL), w_cat)


def _moba_prompt_kernel(far_ref, qt_ref, kb_ref, vt_ref, km_ref, bias_ref, o_ref, sel_scr, qm_scr, *, hg):
    grp = pl.program_id(1)
    i = pl.program_id(2)
    nb = km_ref.shape[0]
    pair_w = 2 * DH_A
    blk = lax.broadcasted_iota(jnp.int32, (nb, MOBA_BLOCK), 0)
    blkf = blk.astype(f32)
    feat = lax.broadcasted_iota(jnp.int32, (pair_w, MOBA_BLOCK), 0)
    is_far = blk <= i - NEAR_TILES
    for hh in range(hg):
        pair, lo = hh // 2, (hh % 2) * DH_A
        qt = qt_ref[pair * pair_w:(pair + 1) * pair_w, :]
        qm = jnp.where(jnp.logical_and(feat >= lo, feat < lo + DH_A), qt, 0.0)
        gate = jnp.dot(km_ref[:, pair * pair_w:(pair + 1) * pair_w], qm, precision=HIGHEST,
                       preferred_element_type=f32)
        g = jnp.where(blk < i, gate, -jnp.inf)
        chosen = blk == i
        for _ in range(MOBA_TOPK):
            m = jnp.max(g, axis=0, keepdims=True)
            first = jnp.min(jnp.where(g == m, blkf, float(nb)), axis=0, keepdims=True)
            hit = jnp.logical_and(blkf == first, m > -jnp.inf)
            chosen = jnp.logical_or(chosen, hit)
            g = jnp.where(hit, -jnp.inf, g)
        sel_scr[hh] = jnp.where(chosen, 0.0, NEG) + jnp.where(is_far, far_ref[grp * hg + hh], 0.0)
        qm_scr[hh] = (qm * DH_A ** -0.5).astype(bf16)

    def step(j, carry, near):
        m, l, acc = carry
        kj = kb_ref[j]
        kj = jnp.stack([kj[:, (hh // 2) * pair_w:(hh // 2 + 1) * pair_w] for hh in range(hg)])
        vj = vt_ref[j].reshape(hg, DH_A, MOBA_BLOCK)
        s = sel_scr[:, pl.ds(j, 1), :] + _bmm(kj, qm_scr[...])
        if near:
            s = s + bias_ref[i - j]
        m_new = jnp.maximum(m, jnp.max(s, axis=1, keepdims=True))
        alpha = jnp.exp(m - m_new)
        p = jnp.exp(s - m_new)
        l = alpha * l + jnp.sum(p, axis=1, keepdims=True)
        acc = alpha * acc + _bmm(vj, p)
        return m_new, l, acc

    init = (jnp.full((hg, 1, MOBA_BLOCK), NEG, f32), jnp.zeros((hg, 1, MOBA_BLOCK), f32),
            jnp.zeros((hg, DH_A, MOBA_BLOCK), f32))
    n_far = jnp.maximum(i - (NEAR_TILES - 1), 0)
    carry = lax.fori_loop(0, n_far, lambda j, c: step(j, c, False), init)
    _, l, acc = lax.fori_loop(n_far, i + 1, lambda j, c: step(j, c, True), carry)
    o_ref[...] = (acc / l).reshape(hg * DH_A, MOBA_BLOCK)


def _moba_prompt(qt, kb, vt, kmean, bias, far):
    bsz, nb = kb.shape[0], kb.shape[1]
    hg = MOBA_HEAD_GROUP
    gw = hg * DH_A
    return pl.pallas_call(
        functools.partial(_moba_prompt_kernel, hg=hg),
        out_shape=jax.ShapeDtypeStruct((bsz * nb, D_A, MOBA_BLOCK), f32),
        grid=(bsz, H_A // hg, nb),
        in_specs=[pl.BlockSpec(memory_space=pltpu.SMEM),
                  pl.BlockSpec((None, gw, MOBA_BLOCK), lambda b, g, i: (b * nb + i, g, 0)),
                  pl.BlockSpec((None, nb, MOBA_BLOCK, gw), lambda b, g, i: (b, 0, 0, g)),
                  pl.BlockSpec((None, nb, gw, MOBA_BLOCK), lambda b, g, i: (b, 0, g, 0)),
                  pl.BlockSpec((None, nb, gw), lambda b, g, i: (b, 0, g)),
                  pl.BlockSpec((NEAR_TILES, hg, MOBA_BLOCK, MOBA_BLOCK), lambda b, g, i: (0, g, 0, 0))],
        out_specs=pl.BlockSpec((None, gw, MOBA_BLOCK), lambda b, g, i: (b * nb + i, g, 0)),
        scratch_shapes=[pltpu.VMEM((hg, nb, MOBA_BLOCK), f32), pltpu.VMEM((hg, 2 * DH_A, MOBA_BLOCK), bf16)],
        compiler_params=_cparams("arbitrary", "arbitrary", "arbitrary"),
        name="moba_prompt",
    )(far, qt, kb, vt, kmean, bias)


def _moba_sample_kernel(pt_ref, q_ref, kn_ref, vn_ref, bias_ref, *refs, n_pages):
    del pt_ref
    kp = refs[:n_pages]
    vp = refs[n_pages:2 * n_pages]
    o_ref = refs[2 * n_pages]
    rows = H_A * SAMPLE_PAD
    ppb = MOBA_BLOCK // PAGE_SIZE
    n_blocks = n_pages // ppb
    rowh = _div_pow2(lax.broadcasted_iota(jnp.int32, (rows, D_A), 0), SAMPLE_PAD)
    laneh = _div_pow2(lax.broadcasted_iota(jnp.int32, (rows, D_A), 1), DH_A)
    own = rowh == laneh
    qe = jnp.where(own, jnp.concatenate([q_ref[...]] * H_A, axis=0), 0.0)
    qeb = (qe * DH_A ** -0.5).astype(bf16)

    lane_f = lax.broadcasted_iota(jnp.int32, (D_A, LANES), 1)
    kmean = jnp.zeros((D_A, LANES), f32)
    for j in range(n_blocks):
        ksum = kp[j * ppb][...]
        for p in range(1, ppb):
            ksum = ksum + kp[j * ppb + p][...]
        kmean = jnp.where(lane_f == j, jnp.sum(ksum, axis=1, keepdims=True) / MOBA_BLOCK, kmean)
    gate = jnp.dot(qe, kmean, precision=HIGHEST, preferred_element_type=f32)
    lane_r = lax.broadcasted_iota(jnp.int32, (rows, LANES), 1)
    lane_rf = lane_r.astype(f32)
    g = jnp.where(lane_r < n_blocks, gate, -jnp.inf)
    chosen = jnp.zeros((rows, LANES), jnp.bool_)
    for _ in range(MOBA_TOPK):
        m = jnp.max(g, axis=1, keepdims=True)
        first = jnp.min(jnp.where(g == m, lane_rf, float(LANES)), axis=1, keepdims=True)
        hit = lane_rf == first
        chosen = jnp.logical_or(chosen, hit)
        g = jnp.where(hit, -jnp.inf, g)
    sel = jnp.where(chosen, 0.0, NEG)

    pad_rows = LANES - SAMPLE_PAD
    kn = jnp.concatenate([kn_ref[...], jnp.zeros((pad_rows, D_A), f32)], axis=0)
    vn = jnp.concatenate([vn_ref[...], jnp.zeros((pad_rows, D_A), f32)], axis=0)
    past = n_pages * PAGE_SIZE
    kt = jnp.concatenate([kp[p][...].astype(bf16) for p in range(n_pages)], axis=1)
    vt = jnp.concatenate([vp[p][...].astype(bf16) for p in range(n_pages)], axis=1)
    sel_past = jnp.concatenate([jnp.broadcast_to(sel[:, j:j + 1], (rows, MOBA_BLOCK)) for j in range(n_blocks)],
                               axis=1)
    s_past = bias_ref[:, :past] + sel_past + jnp.dot(qeb, kt, preferred_element_type=f32)
    s_new = _dot_nt(qeb, kn) + bias_ref[:, past:past + LANES]
    m = jnp.maximum(jnp.max(s_past, axis=-1, keepdims=True), jnp.max(s_new, axis=-1, keepdims=True))
    e_past = jnp.exp(s_past - m)
    e_new = jnp.exp(s_new - m)
    l = jnp.sum(e_past, axis=-1, keepdims=True) + jnp.sum(e_new, axis=-1, keepdims=True)
    acc = lax.dot_general(e_past.astype(bf16), vt, (((1,), (1,)), ((), ())), preferred_element_type=f32)
    acc = acc + _dot(e_new, vn)
    o = jnp.where(own, acc / l, 0.0)
    out = o[0:SAMPLE_PAD]
    for h in range(1, H_A):
        out = out + o[h * SAMPLE_PAD:(h + 1) * SAMPLE_PAD]
    o_ref[...] = out


def _moba_sample(q, kn, vn, cache_k, cache_v, page_table, bias, layer):
    nseq, n_pages = page_table.shape
    assert (n_pages * PAGE_SIZE) % MOBA_BLOCK == 0 and (n_pages * PAGE_SIZE) // MOBA_BLOCK >= MOBA_TOPK
    seq = pl.BlockSpec((None, SAMPLE_PAD, D_A), lambda b, pt: (b, 0, 0))
    page = lambda p: pl.BlockSpec((None, None, D_A, PAGE_SIZE), lambda b, pt: (layer, pt[b, p], 0, 0))
    grid_spec = pltpu.PrefetchScalarGridSpec(
        num_scalar_prefetch=1,
        grid=(nseq,),
        in_specs=[seq, seq, seq, pl.BlockSpec(bias.shape, lambda b, pt: (0, 0))]
        + [page(p) for p in range(n_pages)] * 2,
        out_specs=seq,
    )
    return pl.pallas_call(
        functools.partial(_moba_sample_kernel, n_pages=n_pages),
        out_shape=jax.ShapeDtypeStruct((nseq, SAMPLE_PAD, D_A), f32),
        grid_spec=grid_spec,
        compiler_params=_cparams("arbitrary"),
        name="moba_sample",
    )(page_table, q, kn, vn, bias, *([cache_k] * n_pages), *([cache_v] * n_pages))


def _gdn_kernel(x_ref, ba_ref, z_ref, conv0_ref, s0_ref, cw_ref, avec_ref, dtvec_ref, gon_ref,
                o_ref, s_ref, xs_scr, *, chunk, n_valid):
    t = pl.program_id(1)
    nb, tt = x_ref.shape[0], x_ref.shape[1]

    @pl.when(t == 0)
    def _():
        s_ref[...] = s0_ref[...]
        xs_scr[:, 0:SUBLANES, :] = conv0_ref[...]

    xs_scr[:, SUBLANES:SUBLANES + tt, :] = x_ref[...]

    ci = lax.broadcasted_iota(jnp.int32, (chunk, chunk), 0)
    cj = lax.broadcasted_iota(jnp.int32, (chunk, chunk), 1)
    causal = ci >= cj
    strict = ci > cj
    tri = causal.astype(f32)
    eye = (ci == cj).astype(f32)
    cw = cw_ref[...]
    neg_a = -jnp.exp(avec_ref[...])
    merges = []
    for b in range(int(math.log2(chunk))):
        same_pair = lax.shift_right_logical(ci, b + 1) == lax.shift_right_logical(cj, b + 1)
        row_hi = jnp.bitwise_and(lax.shift_right_logical(ci, b), 1) == 1
        col_lo = jnp.bitwise_and(lax.shift_right_logical(cj, b), 1) == 0
        merges.append(jnp.logical_and(same_pair, jnp.logical_and(row_hi, col_lo)))

    nc = tt // chunk
    qs, ks, vs, zs, bcols, gcols = [], [], [], [], [], []
    base = SUBLANES - (CONV_W - 1)
    for n in range(nb):
        u = xs_scr[n, base:base + tt, :] * cw[0:1]
        for j in range(1, CONV_W):
            u = u + xs_scr[n, base + j:base + j + tt, :] * cw[j:j + 1]
        u = _silu(u)
        ba = ba_ref[n]
        beta = jax.nn.sigmoid(ba)
        g = neg_a * _softplus(ba + dtvec_ref[...])
        if n_valid < tt:
            keep = lax.broadcasted_iota(jnp.int32, (tt, LANES), 0) < n_valid
            beta = jnp.where(keep, beta, 0.0)
            g = jnp.where(keep, g, 0.0)
        for c in range(nc):
            rows = slice(c * chunk, (c + 1) * chunk)
            gcum = jnp.dot(tri, g[rows], precision=HIGHEST, preferred_element_type=f32)
            for h in range(H_B):
                qs.append(u[rows, h * DK:(h + 1) * DK])
                ks.append(u[rows, H_B * DK + h * DK:H_B * DK + (h + 1) * DK])
                vs.append(u[rows, 2 * H_B * DK + h * DV:2 * H_B * DK + (h + 1) * DV])
                zs.append(z_ref[n, rows, h * DV:(h + 1) * DV])
                bcols.append(beta[rows, h:h + 1])
                gcols.append(gcum[:, H_B + h:H_B + h + 1])
    q, k, v, z = jnp.stack(qs), jnp.stack(ks), jnp.stack(vs), jnp.stack(zs)
    bcol, gcol = jnp.stack(bcols), jnp.stack(gcols)
    q = q * lax.rsqrt(jnp.sum(q * q, axis=-1, keepdims=True) + EPS) * DK ** -0.5
    k = k * lax.rsqrt(jnp.sum(k * k, axis=-1, keepdims=True) + EPS)
    grow = jnp.sum(eye[None] * gcol, axis=1, keepdims=True)
    decay = jnp.exp(jnp.where(causal[None], gcol - grow, NEG))
    kb = k * bcol
    lower = jnp.where(strict[None], _bmm_nt(kb, k) * decay, 0.0)
    inv = eye[None] - jnp.where(merges[0][None], lower, 0.0)
    for merge in merges[1:]:
        inv = inv - _bmm(_bmm(inv, jnp.where(merge[None], lower, 0.0)), inv)
    sol = _bmm(inv, jnp.concatenate([v * bcol, kb * jnp.exp(gcol)], axis=2))
    uu, ww = sol[:, :, :DV], sol[:, :, DV:]
    qk = jnp.where(causal[None], _bmm_nt(q, k) * decay, 0.0)
    qe = q * jnp.exp(gcol)
    g_last = gcol[:, chunk - 1:chunk, :]
    ke = k * jnp.exp(g_last - gcol)
    carry = jnp.exp(g_last)

    def at_chunk(a, c):
        return a.reshape((nb, nc, H_B) + a.shape[1:])[:, c].reshape((nb * H_B,) + a.shape[1:])

    state = s_ref[...].reshape(nb * H_B, DK, DV)
    for c in range(nc):
        v_new = at_chunk(uu, c) - _bmm(at_chunk(ww, c), state)
        o = _bmm(at_chunk(qe, c), state) + _bmm(at_chunk(qk, c), v_new)
        state = state * at_chunk(carry, c) + _bmm_tn(at_chunk(ke, c), v_new)
        o = o * lax.rsqrt(jnp.mean(o * o, axis=-1, keepdims=True) + EPS) * gon_ref[...]
        o = o * _silu(at_chunk(z, c))
        for n in range(nb):
            for h in range(H_B):
                o_ref[n, c * chunk:(c + 1) * chunk, h * DV:(h + 1) * DV] = o[n * H_B + h]
    s_ref[...] = state.reshape(nb, H_B, DK, DV)

    xs_scr[:, 0:SUBLANES, :] = xs_scr[:, tt:tt + SUBLANES, :]


def _gdn(xb, ba, zb, conv0, s0, layer, conv_w, a_log, dt_bias, g_onorm, *, nb, tt, chunk, n_valid):
    nseq, t_len = xb.shape[0], xb.shape[1]
    lane_vec = lambda v: jnp.zeros((1, LANES), f32).at[0, H_B:2 * H_B].set(v)
    tok = lambda w: pl.BlockSpec((nb, tt, w), lambda g, t: (g, t, 0))
    const = lambda shape: pl.BlockSpec(shape, lambda g, t: (0,) * len(shape))
    return pl.pallas_call(
        functools.partial(_gdn_kernel, chunk=chunk, n_valid=n_valid),
        out_shape=(jax.ShapeDtypeStruct((nseq, t_len, D_B), f32),
                   jax.ShapeDtypeStruct((nseq, H_B, DK, DV), f32)),
        grid=(nseq // nb, t_len // tt),
        in_specs=[tok(CONV_DIM), tok(LANES), tok(D_B),
                  pl.BlockSpec((None, nb, SUBLANES, CONV_DIM), lambda g, t: (layer, g, 0, 0)),
                  pl.BlockSpec((None, nb, H_B, DK, DV), lambda g, t: (layer, g, 0, 0, 0)),
                  const((CONV_W, CONV_DIM)), const((1, LANES)), const((1, LANES)), const((1, DV))],
        out_specs=(tok(D_B), pl.BlockSpec((nb, H_B, DK, DV), lambda g, t: (g, 0, 0, 0))),
        scratch_shapes=[pltpu.VMEM((nb, tt + SUBLANES, CONV_DIM), f32)],
        compiler_params=_cparams("arbitrary", "arbitrary"),
        name="gdn_%d" % tt,
    )(xb, ba, zb, conv0, s0, conv_w, lane_vec(a_log), lane_vec(dt_bias), g_onorm.reshape(1, DV))


def _outproj_kernel(oa_ref, za_ref, ob_ref, ga_ref, gb_ref, x_ref, gate_ref, wpa_ref, wpb_ref, wout_ref,
                    gpost_ref, y_ref, *, prompt):
    oa = oa_ref[...].T if prompt else oa_ref[...]
    oa = oa * _silu(za_ref[...])
    merged = (jax.nn.sigmoid(ga_ref[...]) * _dot(oa, wpa_ref[...])
              + jax.nn.sigmoid(gb_ref[...]) * _dot(ob_ref[...], wpb_ref[...]))
    y = _dot(merged, wout_ref[...])
    y = y * lax.rsqrt(jnp.mean(y * y, axis=-1, keepdims=True) + EPS) * gpost_ref[...]
    y_ref[...] = x_ref[...] + gate_ref[...] * y


def _outproj(oa, za, ob, ga, gb, x, gate, w_pa, w_pb, w_out, g_post, *, prompt, tiles_per_seq=None):
    n = x.shape[0]
    tm = TOKEN_TILE
    row = lambda i: (i, 0)
    tok = lambda w: pl.BlockSpec((tm, w), row)
    const = lambda shape: pl.BlockSpec(shape, lambda i: (0, 0))
    if prompt:
        oa_spec = pl.BlockSpec((None, D_A, tm), lambda i: (i, 0, 0))
        gate_spec = pl.BlockSpec((None, 1, D_MODEL), lambda i: (i // tiles_per_seq, 0, 0))
    else:
        oa_spec = tok(D_A)
        gate_spec = tok(D_MODEL)
    return pl.pallas_call(
        functools.partial(_outproj_kernel, prompt=prompt),
        out_shape=jax.ShapeDtypeStruct((n, D_MODEL), f32),
        grid=(n // tm,),
        in_specs=[oa_spec, tok(D_A), tok(D_B), tok(D_MODEL), tok(D_MODEL), tok(D_MODEL), gate_spec,
                  const((D_A, D_MODEL)), const((D_B, D_MODEL)), const((D_MODEL, D_MODEL)), const((1, D_MODEL))],
        out_specs=tok(D_MODEL),
        compiler_params=_cparams("arbitrary"),
        name="outproj_prompt" if prompt else "outproj_sample",
    )(oa, za, ob, ga, gb, x, gate, w_pa, w_pb, w_out, g_post.reshape(1, D_MODEL))


def _pack_w_in(w_in):
    depth = w_in.shape[0]
    n_ba = 2 * H_B
    ba = jnp.zeros((depth, D_MODEL, LANES), w_in.dtype).at[:, :, :n_ba].set(w_in[:, :, W_MAIN:W_MAIN + n_ba])
    return jnp.concatenate([w_in[:, :, :W_MAIN], ba, w_in[:, :, W_MAIN + n_ba:]], axis=-1).astype(bf16)


def _prompt_layer(x, mod, lw, bias, far, zero_conv, zero_state):
    bsz, t_len, _ = x.shape
    n = bsz * t_len
    nb = t_len // MOBA_BLOCK
    g_pre, g_post, w_cat, conv_w, a_log, dt_bias, g_onorm, w_pa, w_pb, w_out = lw
    shift, scale, gate = (mod[:, None, j * D_MODEL:(j + 1) * D_MODEL] for j in range(3))
    xf = x.reshape(n, D_MODEL)
    k, v, kb, qt, vt, km, za, xb, zb, ba, ga, gb = _inproj(xf, shift, scale, g_pre, w_cat, prompt=True,
                                                             tiles_per_seq=nb)
    ot = _moba_prompt(qt, kb.reshape(bsz, nb, MOBA_BLOCK, D_A), vt.reshape(bsz, nb, D_A, MOBA_BLOCK),
                      km.reshape(bsz, nb, D_A), bias, far)
    xb3 = xb.reshape(bsz, t_len, CONV_DIM)
    ob, s_new = _gdn(xb3, ba.reshape(bsz, t_len, LANES), zb.reshape(bsz, t_len, D_B), zero_conv, zero_state, 0,
                     conv_w, a_log, dt_bias, g_onorm, nb=1, tt=TOKEN_TILE, chunk=math.gcd(t_len, DELTA_CHUNK),
                     n_valid=TOKEN_TILE)
    y = _outproj(ot, za, ob.reshape(n, D_B), ga, gb, xf, gate, w_pa, w_pb, w_out, g_post, prompt=True,
                 tiles_per_seq=nb)
    heads = lambda a: jnp.transpose(a.reshape(bsz, H_A, DH_A, t_len), (0, 3, 1, 2))
    return y.reshape(bsz, t_len, D_MODEL), heads(k), heads(v), s_new, xb3[:, t_len - (CONV_W - 1):]


def _sample_layer(x, mod, lw, bias, cache_k, cache_v, page_table, state_delta, conv_pad, layer, dec_t):
    nseq = x.shape[0]
    n = nseq * SAMPLE_PAD
    g_pre, g_post, w_cat, conv_w, a_log, dt_bias, g_onorm, w_pa, w_pb, w_out = lw
    per_tok = jnp.repeat(mod, SAMPLE_PAD, axis=0)
    shift, scale, gate = (per_tok[:, j * D_MODEL:(j + 1) * D_MODEL] for j in range(3))
    xf = x.reshape(n, D_MODEL)
    q, k, v, za, xb, zb, ba, ga, gb = _inproj(xf, shift, scale, g_pre, w_cat, prompt=False)
    seq3 = lambda a: a.reshape(nseq, SAMPLE_PAD, a.shape[-1])
    oa = _moba_sample(seq3(q), seq3(k), seq3(v), cache_k, cache_v, page_table, bias, layer)
    xb3 = seq3(xb)
    ob, s_new = _gdn(xb3, seq3(ba), seq3(zb), conv_pad, state_delta, layer, conv_w, a_log, dt_bias, g_onorm,
                     nb=SUBLANES, tt=SAMPLE_PAD, chunk=SAMPLE_PAD, n_valid=dec_t)
    y = _outproj(oa.reshape(n, D_A), za, ob.reshape(n, D_B), ga, gb, xf, gate, w_pa, w_pb, w_out, g_post,
                 prompt=False)
    heads = lambda a: seq3(a)[:, :dec_t].reshape(nseq, dec_t, H_A, DH_A)
    return y.reshape(nseq, SAMPLE_PAD, D_MODEL), heads(k), heads(v), s_new, xb3[:, dec_t - (CONV_W - 1):dec_t]


def kernel(x_prompt, x_sample, c_prompt, c_sample, cache_k, cache_v, state_delta, state_conv, page_table, rel_bias, w_ada, b_ada, g_pre, g_post, w_in, conv_w, a_log, dt_bias, g_onorm, w_pa, w_pb, w_out):
    bsz, seq, _ = x_prompt.shape
    dec_b, dec_t, _ = x_sample.shape
    depth = w_in.shape[0]
    n_pages = page_table.shape[1]
    past_len = n_pages * PAGE_SIZE
    assert seq % MOBA_BLOCK == 0 and seq >= CONV_W - 1 and CONV_W - 1 <= dec_t <= SAMPLE_PAD
    assert dec_b % SUBLANES == 0

    bias_p = _bias_prompt(rel_bias)
    bias_s, far = _bias_sample(rel_bias, past_len, dec_t)
    far = far[:, 0]
    c_all = jnp.concatenate([c_prompt, c_sample], axis=0)
    c_all = jnp.pad(c_all, ((0, -c_all.shape[0] % (2 * SUBLANES)), (0, 0)))
    mod = _modulation(c_all, w_ada, b_ada)
    w_cat = _pack_w_in(w_in)
    w_pa_b, w_pb_b, w_out_b = w_pa.astype(bf16), w_pb.astype(bf16), w_out.astype(bf16)
    ck = jnp.transpose(cache_k, (0, 1, 3, 4, 2)).reshape(cache_k.shape[0], cache_k.shape[1], D_A, PAGE_SIZE)
    cv = jnp.transpose(cache_v, (0, 1, 3, 4, 2)).reshape(cache_v.shape[0], cache_v.shape[1], D_A, PAGE_SIZE)
    conv_pad = jnp.pad(state_conv, ((0, 0), (0, 0), (SUBLANES - (CONV_W - 1), 0), (0, 0)))
    zero_conv = jnp.zeros((1, bsz, SUBLANES, CONV_DIM), f32)
    zero_state = jnp.zeros((1, bsz, H_B, DK, DV), f32)

    hp = x_prompt
    hs = jnp.pad(x_sample, ((0, 0), (0, SAMPLE_PAD - dec_t), (0, 0)))
    outs = [[] for _ in range(8)]
    for l in range(depth):
        lw = (g_pre[l], g_post[l], w_cat[l], conv_w[l], a_log[l], dt_bias[l], g_onorm[l],
              w_pa_b[l], w_pb_b[l], w_out_b[l])
        hp, kp, vp, sp, cp = _prompt_layer(hp, mod[l, :bsz], lw, bias_p, far, zero_conv, zero_state)
        hs, ks, vs, ss, cs = _sample_layer(hs, mod[l, bsz:bsz + dec_b], lw, bias_s, ck, cv, page_table, state_delta,
                                           conv_pad, l, dec_t)
        for lst, val in zip(outs, (kp, vp, ks, vs, sp, ss, cp, cs)):
            lst.append(val)
    return (hp, hs[:, :dec_t]) + tuple(jnp.stack(o) for o in outs)
```

```python
import functools
import math

import jax
import jax.numpy as jnp
from jax import lax
from jax.experimental import pallas as pl
from jax.experimental.pallas import tpu as pltpu

f32 = jnp.float32
bf16 = jnp.bfloat16
HIGHEST = lax.Precision.HIGHEST

D_MODEL = 1024
D_A = D_MODEL // 2
DH_A = 64
H_A = D_A // DH_A
MOBA_BLOCK = 256
MOBA_TOPK = 3
N_BUCKETS = 32
MAX_DISTANCE = 1024
D_B = D_MODEL // 2
DK = 128
DV = 128
H_B = D_B // DV
CONV_W = 4
CONV_DIM = 2 * H_B * DK + H_B * DV
DELTA_CHUNK = 64
PAGE_SIZE = 128
EPS = 1e-6

LANES = 128
SUBLANES = 8
NEG = -1e30
LOG2E = math.log2(math.e)
TOKEN_TILE = MOBA_BLOCK
SAMPLE_PAD = SUBLANES
NEAR_TILES = MAX_DISTANCE // MOBA_BLOCK + 1
MOBA_HEAD_GROUP = 8
W_MAIN = 4 * D_A + CONV_DIM + H_B * DV
W_BA = W_MAIN
W_GATES = W_MAIN + LANES
W_TOTAL = W_GATES + 2 * D_MODEL
VMEM_LIMIT = 48 * 1024 * 1024


def _cparams(*sem):
    return pltpu.CompilerParams(dimension_semantics=sem, vmem_limit_bytes=VMEM_LIMIT)


def _dot(a, b):
    return jnp.dot(a.astype(bf16), b.astype(bf16), preferred_element_type=f32)


def _dot_nt(a, b):
    return lax.dot_general(a.astype(bf16), b.astype(bf16), (((1,), (1,)), ((), ())), preferred_element_type=f32)


def _dot_tn(a, b):
    return lax.dot_general(a.astype(bf16), b.astype(bf16), (((0,), (0,)), ((), ())), preferred_element_type=f32)


def _bmm(a, b):
    return lax.dot_general(a.astype(bf16), b.astype(bf16), (((2,), (1,)), ((0,), (0,))), preferred_element_type=f32)


def _bmm_nt(a, b):
    return lax.dot_general(a.astype(bf16), b.astype(bf16), (((2,), (2,)), ((0,), (0,))), preferred_element_type=f32)


def _bmm_tn(a, b):
    return lax.dot_general(a.astype(bf16), b.astype(bf16), (((1,), (1,)), ((0,), (0,))), preferred_element_type=f32)


def _silu(x):
    return x * jax.nn.sigmoid(x)


def _softplus(x):
    return jnp.maximum(x, 0.0) + jnp.log(1.0 + jnp.exp(-jnp.abs(x)))


def _div_pow2(x, n):
    return lax.shift_right_logical(x, int(math.log2(n)))


def _rel_bucket(dist):
    n = jnp.maximum(dist, 0)
    max_exact = N_BUCKETS // 2
    nf = jnp.maximum(n, 1).astype(f32)
    large = max_exact + (jnp.log(nf / max_exact) / math.log(MAX_DISTANCE / max_exact)
                         * (N_BUCKETS - max_exact)).astype(jnp.int32)
    large = jnp.minimum(large, N_BUCKETS - 1)
    return jnp.where(n < max_exact, n, large)


def _bias_prompt_kernel(rb_ref, o_ref):
    d = pl.program_id(0)
    h = pl.program_id(1)
    r = lax.broadcasted_iota(jnp.int32, (MOBA_BLOCK, MOBA_BLOCK), 0)
    c = lax.broadcasted_iota(jnp.int32, (MOBA_BLOCK, MOBA_BLOCK), 1)
    dist = d * MOBA_BLOCK + c - r
    bucket = _rel_bucket(dist)
    acc = jnp.zeros((MOBA_BLOCK, MOBA_BLOCK), f32)
    for b in range(N_BUCKETS):
        acc = jnp.where(bucket == b, rb_ref[b, h], acc)
    o_ref[...] = jnp.where(dist >= 0, acc * LOG2E, NEG)


def _bias_prompt(rel_bias):
    return pl.pallas_call(
        _bias_prompt_kernel,
        out_shape=jax.ShapeDtypeStruct((NEAR_TILES, H_A, MOBA_BLOCK, MOBA_BLOCK), f32),
        grid=(NEAR_TILES, H_A),
        in_specs=[pl.BlockSpec(memory_space=pltpu.SMEM)],
        out_specs=pl.BlockSpec((None, None, MOBA_BLOCK, MOBA_BLOCK), lambda d, h: (d, h, 0, 0)),
        compiler_params=_cparams("arbitrary", "arbitrary"),
        name="bias_prompt",
    )(rel_bias)


def _bias_sample_kernel(rb_ref, o_ref, far_ref, *, past_len, dec_t):
    rows = H_A * SAMPLE_PAD
    width = past_len + LANES
    row = lax.broadcasted_iota(jnp.int32, (rows, width), 0)
    lane = lax.broadcasted_iota(jnp.int32, (rows, width), 1)
    rowh = _div_pow2(row, SAMPLE_PAD)
    tq = jnp.minimum(row - rowh * SAMPLE_PAD, dec_t - 1)
    tn = lane - past_len
    is_new = lane >= past_len
    dist = jnp.where(is_new, tq - tn, past_len + tq - lane)
    valid = jnp.logical_or(jnp.logical_not(is_new), jnp.logical_and(tn <= tq, tn < dec_t))
    bucket = _rel_bucket(dist)
    far_bucket = _rel_bucket(jnp.full((H_A, LANES), MAX_DISTANCE, jnp.int32))
    far_row = lax.broadcasted_iota(jnp.int32, (H_A, LANES), 0)

    def body(b, carry):
        acc, far = carry
        for h in range(H_A):
            v = rb_ref[b, h]
            acc = jnp.where(jnp.logical_and(bucket == b, rowh == h), v, acc)
            far = jnp.where(jnp.logical_and(far_bucket == b, far_row == h), v, far)
        return acc, far

    acc, far = lax.fori_loop(0, N_BUCKETS, body, (jnp.zeros((rows, width), f32), jnp.zeros((H_A, LANES), f32)))
    o_ref[...] = jnp.where(valid, acc, NEG)
    far_ref[...] = far * LOG2E


def _bias_sample(rel_bias, past_len, dec_t):
    return pl.pallas_call(
        functools.partial(_bias_sample_kernel, past_len=past_len, dec_t=dec_t),
        out_shape=(jax.ShapeDtypeStruct((H_A * SAMPLE_PAD, past_len + LANES), f32),
                   jax.ShapeDtypeStruct((H_A, LANES), f32)),
        in_specs=[pl.BlockSpec(memory_space=pltpu.SMEM)],
        name="bias_sample",
    )(rel_bias)


def _mod_kernel(c_ref, w_ref, b_ref, o_ref):
    o_ref[...] = _dot(_silu(c_ref[...]), w_ref[...]) + b_ref[...]


def _modulation(c_all, w_ada, b_ada):
    depth = w_ada.shape[0]
    rows = c_all.shape[0]
    return pl.pallas_call(
        _mod_kernel,
        out_shape=jax.ShapeDtypeStruct((depth, rows, 3 * D_MODEL), f32),
        grid=(depth, 3),
        in_specs=[pl.BlockSpec((rows, D_MODEL), lambda l, j: (0, 0)),
                  pl.BlockSpec((None, D_MODEL, D_MODEL), lambda l, j: (l, 0, j)),
                  pl.BlockSpec((None, 1, D_MODEL), lambda l, j: (l, 0, j))],
        out_specs=pl.BlockSpec((None, rows, D_MODEL), lambda l, j: (l, 0, j)),
        compiler_params=_cparams("arbitrary", "arbitrary"),
        name="modulation",
    )(c_all, w_ada, b_ada.reshape(depth, 1, 3 * D_MODEL))


def _inproj_kernel(x_ref, shift_ref, scale_ref, gpre_ref, w_ref, *outs, prompt):
    x = x_ref[...]
    h = x * lax.rsqrt(jnp.mean(x * x, axis=-1, keepdims=True) + EPS) * gpre_ref[...]
    hb = (h * (1.0 + scale_ref[...]) + shift_ref[...]).astype(bf16)

    def proj(lo, hi):
        return jnp.dot(hb, w_ref[:, lo:hi], preferred_element_type=f32)

    q = proj(0, D_A)
    k = proj(D_A, 2 * D_A)
    v = proj(2 * D_A, 3 * D_A)
    if prompt:
        k_ref, v_ref, kb_ref, qt_ref, vt_ref, km_ref, za_ref, xb_ref, zb_ref, ba_ref, ga_ref, gb_ref = outs
        vt = v.T
        k_ref[...] = k.T
        v_ref[...] = vt
        kb_ref[...] = k.astype(bf16)
        qt_ref[...] = q.T
        vt_ref[...] = vt.astype(bf16)
        km_ref[...] = jnp.mean(k, axis=0, keepdims=True)
    else:
        q_ref, k_ref, v_ref, za_ref, xb_ref, zb_ref, ba_ref, ga_ref, gb_ref = outs
        q_ref[...] = q
        k_ref[...] = k
        v_ref[...] = v
    za_ref[...] = proj(3 * D_A, 4 * D_A).astype(za_ref.dtype)
    xb_ref[...] = proj(4 * D_A, 4 * D_A + CONV_DIM)
    zb_ref[...] = proj(4 * D_A + CONV_DIM, W_MAIN).astype(zb_ref.dtype)
    ba_ref[...] = proj(W_BA, W_BA + LANES)
    ga_ref[...] = proj(W_GATES, W_GATES + D_MODEL).astype(ga_ref.dtype)
    gb_ref[...] = proj(W_GATES + D_MODEL, W_TOTAL).astype(gb_ref.dtype)


def _inproj(x, shift, scale, g_pre, w_cat, *, prompt, tiles_per_seq=None):
    n = x.shape[0]
    tm = TOKEN_TILE
    nt = n // tm
    row = lambda i: (i, 0)
    if prompt:
        mod_spec = pl.BlockSpec((None, 1, D_MODEL), lambda i: (i // tiles_per_seq, 0, 0))
    else:
        mod_spec = pl.BlockSpec((tm, D_MODEL), row)
    tok = lambda w, dt=f32: (jax.ShapeDtypeStruct((n, w), dt), pl.BlockSpec((tm, w), row))
    gdt = bf16 if prompt else f32
    common = [tok(D_A, gdt), tok(CONV_DIM), tok(D_B, gdt), tok(LANES), tok(D_MODEL, gdt), tok(D_MODEL, gdt)]
    if prompt:
        seq_t = (jax.ShapeDtypeStruct((nt // tiles_per_seq, D_A, tiles_per_seq * tm), f32),
                 pl.BlockSpec((None, D_A, tm), lambda i: (i // tiles_per_seq, 0, i % tiles_per_seq)))
        outs = [seq_t, seq_t, tok(D_A, bf16),
                (jax.ShapeDtypeStruct((nt, D_A, tm), f32), pl.BlockSpec((None, D_A, tm), lambda i: (i, 0, 0))),
                (jax.ShapeDtypeStruct((nt, D_A, tm), bf16), pl.BlockSpec((None, D_A, tm), lambda i: (i, 0, 0))),
                (jax.ShapeDtypeStruct((nt, 1, D_A), f32), pl.BlockSpec((None, 1, D_A), lambda i: (i, 0, 0)))] + common
    else:
        outs = [tok(D_A), tok(D_A), tok(D_A)] + common
    return pl.pallas_call(
        functools.partial(_inproj_kernel, prompt=prompt),
        out_shape=tuple(o[0] for o in outs),
        grid=(nt,),
        in_specs=[pl.BlockSpec((tm, D_MODEL), row), mod_spec, mod_spec,
                  pl.BlockSpec((1, D_MODEL), lambda i: (0, 0)),
                  pl.BlockSpec((D_MODEL, W_TOTAL), lambda i: (0, 0), pipeline_mode=pl.Buffered(1))],
        out_specs=tuple(o[1] for o in outs),
        compiler_params=_cparams("arbitrary"),
        name="inproj_prompt" if prompt else "inproj_sample",
    )(x, shift, scale, g_pre.reshape(1, D_MODEL), w_cat)


def _moba_prompt_kernel(far_ref, qt_ref, kb_ref, vt_ref, km_ref, bias_ref, o_ref, sel_scr, far_scr, qm_scr, *, hg):
    grp = pl.program_id(1)
    i = pl.program_id(2)
    nb = km_ref.shape[0]
    pair_w = 2 * DH_A
    blk = lax.broadcasted_iota(jnp.int32, (nb, MOBA_BLOCK), 0)
    blkf = blk.astype(f32)
    feat = lax.broadcasted_iota(jnp.int32, (pair_w, MOBA_BLOCK), 0)
    for hh in range(hg):
        pair, lo = hh // 2, (hh % 2) * DH_A
        qt = qt_ref[pair * pair_w:(pair + 1) * pair_w, :]
        qm = jnp.where(jnp.logical_and(feat >= lo, feat < lo + DH_A), qt, 0.0)
        gate = jnp.dot(km_ref[:, pair * pair_w:(pair + 1) * pair_w], qm, precision=HIGHEST,
                       preferred_element_type=f32)
        g = jnp.where(blk < i, gate, -jnp.inf)
        chosen = blk == i
        for _ in range(MOBA_TOPK):
            m = jnp.max(g, axis=0, keepdims=True)
            first = jnp.min(jnp.where(g == m, blkf, float(nb)), axis=0, keepdims=True)
            hit = jnp.logical_and(blkf == first, m > -jnp.inf)
            chosen = jnp.logical_or(chosen, hit)
            g = jnp.where(hit, -jnp.inf, g)
        sel_scr[hh] = chosen.astype(f32)
        far_scr[hh] = jnp.full((1, MOBA_BLOCK), far_ref[grp * hg + hh], f32)
        qm_scr[hh] = (qm * (DH_A ** -0.5 * LOG2E)).astype(bf16)

    def step(j, carry, near):
        m, l, acc = carry
        kj = kb_ref[j]
        kj = jnp.stack([kj[:, (hh // 2) * pair_w:(hh // 2 + 1) * pair_w] for hh in range(hg)])
        vj = vt_ref[j].reshape(hg, DH_A, MOBA_BLOCK)
        s = _bmm(kj, qm_scr[...])
        if near:
            s = bias_ref[i - j] + s
            shift = 0.0
        else:
            shift = far_scr[...]
        ok = sel_scr[:, pl.ds(j, 1), :] > 0.5
        m_new = jnp.where(ok, jnp.maximum(m, jnp.max(s, axis=1, keepdims=True) + shift), m)
        alpha = jnp.exp2(m - m_new)
        p = jnp.exp2(s - jnp.where(ok, m_new - shift, -NEG))
        l = alpha * l + jnp.sum(p, axis=1, keepdims=True)
        acc = alpha * acc + _bmm(vj, p)
        return m_new, l, acc

    init = (jnp.full((hg, 1, MOBA_BLOCK), NEG, f32), jnp.zeros((hg, 1, MOBA_BLOCK), f32),
            jnp.zeros((hg, DH_A, MOBA_BLOCK), f32))
    n_far = jnp.maximum(i - (NEAR_TILES - 1), 0)
    carry = lax.fori_loop(0, n_far, lambda j, c: step(j, c, False), init)
    _, l, acc = lax.fori_loop(n_far, i + 1, lambda j, c: step(j, c, True), carry)
    o_ref[...] = (acc / l).reshape(hg * DH_A, MOBA_BLOCK)


def _moba_prompt(qt, kb, vt, kmean, bias, far):
    bsz, nb = kb.shape[0], kb.shape[1]
    hg = MOBA_HEAD_GROUP
    gw = hg * DH_A
    return pl.pallas_call(
        functools.partial(_moba_prompt_kernel, hg=hg),
        out_shape=jax.ShapeDtypeStruct((bsz * nb, D_A, MOBA_BLOCK), f32),
        grid=(bsz, H_A // hg, nb),
        in_specs=[pl.BlockSpec(memory_space=pltpu.SMEM),
                  pl.BlockSpec((None, gw, MOBA_BLOCK), lambda b, g, i: (b * nb + i, g, 0)),
                  pl.BlockSpec((None, nb, MOBA_BLOCK, gw), lambda b, g, i: (b, 0, 0, g)),
                  pl.BlockSpec((None, nb, gw, MOBA_BLOCK), lambda b, g, i: (b, 0, g, 0)),
                  pl.BlockSpec((None, nb, gw), lambda b, g, i: (b, 0, g)),
                  pl.BlockSpec((NEAR_TILES, hg, MOBA_BLOCK, MOBA_BLOCK), lambda b, g, i: (0, g, 0, 0))],
        out_specs=pl.BlockSpec((None, gw, MOBA_BLOCK), lambda b, g, i: (b * nb + i, g, 0)),
        scratch_shapes=[pltpu.VMEM((hg, nb, MOBA_BLOCK), f32), pltpu.VMEM((hg, 1, MOBA_BLOCK), f32),
                        pltpu.VMEM((hg, 2 * DH_A, MOBA_BLOCK), bf16)],
        compiler_params=_cparams("arbitrary", "arbitrary", "arbitrary"),
        name="moba_prompt",
    )(far, qt, kb, vt, kmean, bias)


def _moba_sample_kernel(pt_ref, q_ref, kn_ref, vn_ref, bias_ref, *refs, n_pages):
    del pt_ref
    kp = refs[:n_pages]
    vp = refs[n_pages:2 * n_pages]
    o_ref = refs[2 * n_pages]
    rows = H_A * SAMPLE_PAD
    ppb = MOBA_BLOCK // PAGE_SIZE
    n_blocks = n_pages // ppb
    rowh = _div_pow2(lax.broadcasted_iota(jnp.int32, (rows, D_A), 0), SAMPLE_PAD)
    laneh = _div_pow2(lax.broadcasted_iota(jnp.int32, (rows, D_A), 1), DH_A)
    own = rowh == laneh
    qe = jnp.where(own, jnp.concatenate([q_ref[...]] * H_A, axis=0), 0.0)
    qeb = (qe * DH_A ** -0.5).astype(bf16)

    lane_f = lax.broadcasted_iota(jnp.int32, (D_A, LANES), 1)
    kmean = jnp.zeros((D_A, LANES), f32)
    for j in range(n_blocks):
        ksum = kp[j * ppb][...]
        for p in range(1, ppb):
            ksum = ksum + kp[j * ppb + p][...]
        kmean = jnp.where(lane_f == j, jnp.sum(ksum, axis=1, keepdims=True) / MOBA_BLOCK, kmean)
    gate = jnp.dot(qe, kmean, precision=HIGHEST, preferred_element_type=f32)
    lane_r = lax.broadcasted_iota(jnp.int32, (rows, LANES), 1)
    lane_rf = lane_r.astype(f32)
    g = jnp.where(lane_r < n_blocks, gate, -jnp.inf)
    chosen = jnp.zeros((rows, LANES), jnp.bool_)
    for _ in range(MOBA_TOPK):
        m = jnp.max(g, axis=1, keepdims=True)
        first = jnp.min(jnp.where(g == m, lane_rf, float(LANES)), axis=1, keepdims=True)
        hit = lane_rf == first
        chosen = jnp.logical_or(chosen, hit)
        g = jnp.where(hit, -jnp.inf, g)
    sel = jnp.where(chosen, 0.0, NEG)

    pad_rows = LANES - SAMPLE_PAD
    kn = jnp.concatenate([kn_ref[...], jnp.zeros((pad_rows, D_A), f32)], axis=0)
    vn = jnp.concatenate([vn_ref[...], jnp.zeros((pad_rows, D_A), f32)], axis=0)
    past = n_pages * PAGE_SIZE
    kt = jnp.concatenate([kp[p][...].astype(bf16) for p in range(n_pages)], axis=1)
    vt = jnp.concatenate([vp[p][...].astype(bf16) for p in range(n_pages)], axis=1)
    sel_past = jnp.concatenate([jnp.broadcast_to(sel[:, j:j + 1], (rows, MOBA_BLOCK)) for j in range(n_blocks)],
                               axis=1)
    s_past = bias_ref[:, :past] + sel_past + jnp.dot(qeb, kt, preferred_element_type=f32)
    s_new = _dot_nt(qeb, kn) + bias_ref[:, past:past + LANES]
    m = jnp.maximum(jnp.max(s_past, axis=-1, keepdims=True), jnp.max(s_new, axis=-1, keepdims=True))
    e_past = jnp.exp(s_past - m)
    e_new = jnp.exp(s_new - m)
    l = jnp.sum(e_past, axis=-1, keepdims=True) + jnp.sum(e_new, axis=-1, keepdims=True)
    acc = lax.dot_general(e_past.astype(bf16), vt, (((1,), (1,)), ((), ())), preferred_element_type=f32)
    acc = acc + _dot(e_new, vn)
    o = jnp.where(own, acc / l, 0.0)
    out = o[0:SAMPLE_PAD]
    for h in range(1, H_A):
        out = out + o[h * SAMPLE_PAD:(h + 1) * SAMPLE_PAD]
    o_ref[...] = out


def _moba_sample(q, kn, vn, cache_k, cache_v, page_table, bias, layer):
    nseq, n_pages = page_table.shape
    assert (n_pages * PAGE_SIZE) % MOBA_BLOCK == 0 and (n_pages * PAGE_SIZE) // MOBA_BLOCK >= MOBA_TOPK
    seq = pl.BlockSpec((None, SAMPLE_PAD, D_A), lambda b, pt: (b, 0, 0))
    page = lambda p: pl.BlockSpec((None, None, D_A, PAGE_SIZE), lambda b, pt: (layer, pt[b, p], 0, 0))
    grid_spec = pltpu.PrefetchScalarGridSpec(
        num_scalar_prefetch=1,
        grid=(nseq,),
        in_specs=[seq, seq, seq, pl.BlockSpec(bias.shape, lambda b, pt: (0, 0))]
        + [page(p) for p in range(n_pages)] * 2,
        out_specs=seq,
    )
    return pl.pallas_call(
        functools.partial(_moba_sample_kernel, n_pages=n_pages),
        out_shape=jax.ShapeDtypeStruct((nseq, SAMPLE_PAD, D_A), f32),
        grid_spec=grid_spec,
        compiler_params=_cparams("arbitrary"),
        name="moba_sample",
    )(page_table, q, kn, vn, bias, *([cache_k] * n_pages), *([cache_v] * n_pages))


def _gdn_kernel(x_ref, ba_ref, z_ref, conv0_ref, s0_ref, cw_ref, avec_ref, dtvec_ref, gon_ref,
                o_ref, s_ref, xs_scr, *, chunk, n_valid):
    t = pl.program_id(1)
    nb, tt = x_ref.shape[0], x_ref.shape[1]

    @pl.when(t == 0)
    def _():
        s_ref[...] = s0_ref[...]
        xs_scr[:, 0:SUBLANES, :] = conv0_ref[...]

    xs_scr[:, SUBLANES:SUBLANES + tt, :] = x_ref[...]

    ci = lax.broadcasted_iota(jnp.int32, (chunk, chunk), 0)
    cj = lax.broadcasted_iota(jnp.int32, (chunk, chunk), 1)
    causal = ci >= cj
    strict = ci > cj
    tri = causal.astype(f32)
    eye = (ci == cj).astype(f32)
    cw = cw_ref[...]
    neg_a = -jnp.exp(avec_ref[...])
    merges = []
    for b in range(int(math.log2(chunk))):
        same_pair = lax.shift_right_logical(ci, b + 1) == lax.shift_right_logical(cj, b + 1)
        row_hi = jnp.bitwise_and(lax.shift_right_logical(ci, b), 1) == 1
        col_lo = jnp.bitwise_and(lax.shift_right_logical(cj, b), 1) == 0
        merges.append(jnp.logical_and(same_pair, jnp.logical_and(row_hi, col_lo)))

    nc = tt // chunk
    qs, ks, vs, zs, bcols, gcols = [], [], [], [], [], []
    base = SUBLANES - (CONV_W - 1)
    for n in range(nb):
        u = xs_scr[n, base:base + tt, :] * cw[0:1]
        for j in range(1, CONV_W):
            u = u + xs_scr[n, base + j:base + j + tt, :] * cw[j:j + 1]
        u = _silu(u)
        ba = ba_ref[n]
        beta = jax.nn.sigmoid(ba)
        g = neg_a * _softplus(ba + dtvec_ref[...])
        if n_valid < tt:
            keep = lax.broadcasted_iota(jnp.int32, (tt, LANES), 0) < n_valid
            beta = jnp.where(keep, beta, 0.0)
            g = jnp.where(keep, g, 0.0)
        for c in range(nc):
            rows = slice(c * chunk, (c + 1) * chunk)
            gcum = jnp.dot(tri, g[rows], precision=HIGHEST, preferred_element_type=f32)
            for h in range(H_B):
                qs.append(u[rows, h * DK:(h + 1) * DK])
                ks.append(u[rows, H_B * DK + h * DK:H_B * DK + (h + 1) * DK])
                vs.append(u[rows, 2 * H_B * DK + h * DV:2 * H_B * DK + (h + 1) * DV])
                zs.append(z_ref[n, rows, h * DV:(h + 1) * DV])
                bcols.append(beta[rows, h:h + 1])
                gcols.append(gcum[:, H_B + h:H_B + h + 1])
    q, k, v, z = jnp.stack(qs), jnp.stack(ks), jnp.stack(vs), jnp.stack(zs)
    bcol, gcol = jnp.stack(bcols), jnp.stack(gcols)
    q = q * lax.rsqrt(jnp.sum(q * q, axis=-1, keepdims=True) + EPS) * DK ** -0.5
    k = k * lax.rsqrt(jnp.sum(k * k, axis=-1, keepdims=True) + EPS)
    grow = jnp.sum(eye[None] * gcol, axis=1, keepdims=True)
    decay = jnp.exp(jnp.where(causal[None], gcol - grow, NEG))
    kb = k * bcol
    lower = jnp.where(strict[None], _bmm_nt(kb, k) * decay, 0.0)
    inv = eye[None] - jnp.where(merges[0][None], lower, 0.0)
    for merge in merges[1:]:
        inv = inv - _bmm(_bmm(inv, jnp.where(merge[None], lower, 0.0)), inv)
    sol = _bmm(inv, jnp.concatenate([v * bcol, kb * jnp.exp(gcol)], axis=2))
    uu, ww = sol[:, :, :DV], sol[:, :, DV:]
    qk = jnp.where(causal[None], _bmm_nt(q, k) * decay, 0.0)
    qe = q * jnp.exp(gcol)
    g_last = gcol[:, chunk - 1:chunk, :]
    ke = k * jnp.exp(g_last - gcol)
    carry = jnp.exp(g_last)

    def at_chunk(a, c):
        return a.reshape((nb, nc, H_B) + a.shape[1:])[:, c].reshape((nb * H_B,) + a.shape[1:])

    state = s_ref[...].reshape(nb * H_B, DK, DV)
    for c in range(nc):
        v_new = at_chunk(uu, c) - _bmm(at_chunk(ww, c), state)
        o = _bmm(at_chunk(qe, c), state) + _bmm(at_chunk(qk, c), v_new)
        state = state * at_chunk(carry, c) + _bmm_tn(at_chunk(ke, c), v_new)
        o = o * lax.rsqrt(jnp.mean(o * o, axis=-1, keepdims=True) + EPS) * gon_ref[...]
        o = o * _silu(at_chunk(z, c))
        for n in range(nb):
            for h in range(H_B):
                o_ref[n, c * chunk:(c + 1) * chunk, h * DV:(h + 1) * DV] = o[n * H_B + h].astype(o_ref.dtype)
    s_ref[...] = state.reshape(nb, H_B, DK, DV)

    xs_scr[:, 0:SUBLANES, :] = xs_scr[:, tt:tt + SUBLANES, :]


def _gdn(xb, ba, zb, conv0, s0, layer, conv_w, a_log, dt_bias, g_onorm, *, nb, tt, chunk, n_valid):
    nseq, t_len = xb.shape[0], xb.shape[1]
    lane_vec = lambda v: jnp.zeros((1, LANES), f32).at[0, H_B:2 * H_B].set(v)
    tok = lambda w: pl.BlockSpec((nb, tt, w), lambda g, t: (g, t, 0))
    const = lambda shape: pl.BlockSpec(shape, lambda g, t: (0,) * len(shape))
    return pl.pallas_call(
        functools.partial(_gdn_kernel, chunk=chunk, n_valid=n_valid),
        out_shape=(jax.ShapeDtypeStruct((nseq, t_len, D_B), zb.dtype),
                   jax.ShapeDtypeStruct((nseq, H_B, DK, DV), f32)),
        grid=(nseq // nb, t_len // tt),
        in_specs=[tok(CONV_DIM), tok(LANES), tok(D_B),
                  pl.BlockSpec((None, nb, SUBLANES, CONV_DIM), lambda g, t: (layer, g, 0, 0)),
                  pl.BlockSpec((None, nb, H_B, DK, DV), lambda g, t: (layer, g, 0, 0, 0)),
                  const((CONV_W, CONV_DIM)), const((1, LANES)), const((1, LANES)), const((1, DV))],
        out_specs=(tok(D_B), pl.BlockSpec((nb, H_B, DK, DV), lambda g, t: (g, 0, 0, 0))),
        scratch_shapes=[pltpu.VMEM((nb, tt + SUBLANES, CONV_DIM), f32)],
        compiler_params=_cparams("arbitrary", "arbitrary"),
        name="gdn_%d" % tt,
    )(xb, ba, zb, conv0, s0, conv_w, lane_vec(a_log), lane_vec(dt_bias), g_onorm.reshape(1, DV))


def _outproj_kernel(oa_ref, za_ref, ob_ref, ga_ref, gb_ref, x_ref, gate_ref, wpa_ref, wpb_ref, wout_ref,
                    gpost_ref, y_ref, *, prompt):
    oa = oa_ref[...].T if prompt else oa_ref[...]
    oa = oa * _silu(za_ref[...])
    merged = (jax.nn.sigmoid(ga_ref[...]) * _dot(oa, wpa_ref[...])
              + jax.nn.sigmoid(gb_ref[...]) * _dot(ob_ref[...], wpb_ref[...]))
    y = _dot(merged, wout_ref[...])
    y = y * lax.rsqrt(jnp.mean(y * y, axis=-1, keepdims=True) + EPS) * gpost_ref[...]
    y_ref[...] = x_ref[...] + gate_ref[...] * y


def _outproj(oa, za, ob, ga, gb, x, gate, w_pa, w_pb, w_out, g_post, *, prompt, tiles_per_seq=None):
    n = x.shape[0]
    tm = TOKEN_TILE
    row = lambda i: (i, 0)
    tok = lambda w: pl.BlockSpec((tm, w), row)
    const = lambda shape: pl.BlockSpec(shape, lambda i: (0, 0))
    if prompt:
        oa_spec = pl.BlockSpec((None, D_A, tm), lambda i: (i, 0, 0))
        gate_spec = pl.BlockSpec((None, 1, D_MODEL), lambda i: (i // tiles_per_seq, 0, 0))
    else:
        oa_spec = tok(D_A)
        gate_spec = tok(D_MODEL)
    return pl.pallas_call(
        functools.partial(_outproj_kernel, prompt=prompt),
        out_shape=jax.ShapeDtypeStruct((n, D_MODEL), f32),
        grid=(n // tm,),
        in_specs=[oa_spec, tok(D_A), tok(D_B), tok(D_MODEL), tok(D_MODEL), tok(D_MODEL), gate_spec,
                  const((D_A, D_MODEL)), const((D_B, D_MODEL)), const((D_MODEL, D_MODEL)), const((1, D_MODEL))],
        out_specs=tok(D_MODEL),
        compiler_params=_cparams("arbitrary"),
        name="outproj_prompt" if prompt else "outproj_sample",
    )(oa, za, ob, ga, gb, x, gate, w_pa, w_pb, w_out, g_post.reshape(1, D_MODEL))


def _pack_w_in(w_in):
    depth = w_in.shape[0]
    n_ba = 2 * H_B
    ba = jnp.zeros((depth, D_MODEL, LANES), w_in.dtype).at[:, :, :n_ba].set(w_in[:, :, W_MAIN:W_MAIN + n_ba])
    return jnp.concatenate([w_in[:, :, :W_MAIN], ba, w_in[:, :, W_MAIN + n_ba:]], axis=-1).astype(bf16)


def _prompt_layer(x, mod, lw, bias, far, zero_conv, zero_state):
    bsz, t_len, _ = x.shape
    n = bsz * t_len
    nb = t_len // MOBA_BLOCK
    g_pre, g_post, w_cat, conv_w, a_log, dt_bias, g_onorm, w_pa, w_pb, w_out = lw
    shift, scale, gate = (mod[:, None, j * D_MODEL:(j + 1) * D_MODEL] for j in range(3))
    xf = x.reshape(n, D_MODEL)
    k, v, kb, qt, vt, km, za, xb, zb, ba, ga, gb = _inproj(xf, shift, scale, g_pre, w_cat, prompt=True,
                                                             tiles_per_seq=nb)
    ot = _moba_prompt(qt, kb.reshape(bsz, nb, MOBA_BLOCK, D_A), vt.reshape(bsz, nb, D_A, MOBA_BLOCK),
                      km.reshape(bsz, nb, D_A), bias, far)
    xb3 = xb.reshape(bsz, t_len, CONV_DIM)
    ob, s_new = _gdn(xb3, ba.reshape(bsz, t_len, LANES), zb.reshape(bsz, t_len, D_B), zero_conv, zero_state, 0,
                     conv_w, a_log, dt_bias, g_onorm, nb=1, tt=TOKEN_TILE, chunk=math.gcd(t_len, DELTA_CHUNK),
                     n_valid=TOKEN_TILE)
    y = _outproj(ot, za, ob.reshape(n, D_B), ga, gb, xf, gate, w_pa, w_pb, w_out, g_post, prompt=True,
                 tiles_per_seq=nb)
    heads = lambda a: jnp.transpose(a.reshape(bsz, H_A, DH_A, t_len), (0, 3, 1, 2))
    return y.reshape(bsz, t_len, D_MODEL), heads(k), heads(v), s_new, xb3[:, t_len - (CONV_W - 1):]


def _sample_layer(x, mod, lw, bias, cache_k, cache_v, page_table, state_delta, conv_pad, layer, dec_t):
    nseq = x.shape[0]
    n = nseq * SAMPLE_PAD
    g_pre, g_post, w_cat, conv_w, a_log, dt_bias, g_onorm, w_pa, w_pb, w_out = lw
    per_tok = jnp.repeat(mod, SAMPLE_PAD, axis=0)
    shift, scale, gate = (per_tok[:, j * D_MODEL:(j + 1) * D_MODEL] for j in range(3))
    xf = x.reshape(n, D_MODEL)
    q, k, v, za, xb, zb, ba, ga, gb = _inproj(xf, shift, scale, g_pre, w_cat, prompt=False)
    seq3 = lambda a: a.reshape(nseq, SAMPLE_PAD, a.shape[-1])
    oa = _moba_sample(seq3(q), seq3(k), seq3(v), cache_k, cache_v, page_table, bias, layer)
    xb3 = seq3(xb)
    ob, s_new = _gdn(xb3, seq3(ba), seq3(zb), conv_pad, state_delta, layer, conv_w, a_log, dt_bias, g_onorm,
                     nb=SUBLANES, tt=SAMPLE_PAD, chunk=SAMPLE_PAD, n_valid=dec_t)
    y = _outproj(oa.reshape(n, D_A), za, ob.reshape(n, D_B), ga, gb, xf, gate, w_pa, w_pb, w_out, g_post,
                 prompt=False)
    heads = lambda a: seq3(a)[:, :dec_t].reshape(nseq, dec_t, H_A, DH_A)
    return y.reshape(nseq, SAMPLE_PAD, D_MODEL), heads(k), heads(v), s_new, xb3[:, dec_t - (CONV_W - 1):dec_t]


def kernel(x_prompt, x_sample, c_prompt, c_sample, cache_k, cache_v, state_delta, state_conv, page_table, rel_bias, w_ada, b_ada, g_pre, g_post, w_in, conv_w, a_log, dt_bias, g_onorm, w_pa, w_pb, w_out):
    bsz, seq, _ = x_prompt.shape
    dec_b, dec_t, _ = x_sample.shape
    depth = w_in.shape[0]
    n_pages = page_table.shape[1]
    past_len = n_pages * PAGE_SIZE
    assert seq % MOBA_BLOCK == 0 and seq >= CONV_W - 1 and CONV_W - 1 <= dec_t <= SAMPLE_PAD
    assert dec_b % SUBLANES == 0

    bias_p = _bias_prompt(rel_bias)
    bias_s, far = _bias_sample(rel_bias, past_len, dec_t)
    far = far[:, 0]
    c_all = jnp.concatenate([c_prompt, c_sample], axis=0)
    c_all = jnp.pad(c_all, ((0, -c_all.shape[0] % (2 * SUBLANES)), (0, 0)))
    mod = _modulation(c_all, w_ada, b_ada)
    w_cat = _pack_w_in(w_in)
    w_pa_b, w_pb_b, w_out_b = w_pa.astype(bf16), w_pb.astype(bf16), w_out.astype(bf16)
    ck = jnp.transpose(cache_k, (0, 1, 3, 4, 2)).reshape(cache_k.shape[0], cache_k.shape[1], D_A, PAGE_SIZE)
    cv = jnp.transpose(cache_v, (0, 1, 3, 4, 2)).reshape(cache_v.shape[0], cache_v.shape[1], D_A, PAGE_SIZE)
    conv_pad = jnp.pad(state_conv, ((0, 0), (0, 0), (SUBLANES - (CONV_W - 1), 0), (0, 0)))
    zero_conv = jnp.zeros((1, bsz, SUBLANES, CONV_DIM), f32)
    zero_state = jnp.zeros((1, bsz, H_B, DK, DV), f32)

    hp = x_prompt
    hs = jnp.pad(x_sample, ((0, 0), (0, SAMPLE_PAD - dec_t), (0, 0)))
    outs = [[] for _ in range(8)]
    for l in range(depth):
        lw = (g_pre[l], g_post[l], w_cat[l], conv_w[l], a_log[l], dt_bias[l], g_onorm[l],
              w_pa_b[l], w_pb_b[l], w_out_b[l])
        hp, kp, vp, sp, cp = _prompt_layer(hp, mod[l, :bsz], lw, bias_p, far, zero_conv, zero_state)
        hs, ks, vs, ss, cs = _sample_layer(hs, mod[l, bsz:bsz + dec_b], lw, bias_s, ck, cv, page_table, state_delta,
                                           conv_pad, l, dec_t)
        for lst, val in zip(outs, (kp, vp, ks, vs, sp, ss, cp, cs)):
            lst.append(val)
    return (hp, hs[:, :dec_t]) + tuple(jnp.stack(o) for o in outs)
```

```python
import functools
import math

import jax
import jax.numpy as jnp
from jax import lax
from jax.experimental import pallas as pl
from jax.experimental.pallas import tpu as pltpu

f32 = jnp.float32
bf16 = jnp.bfloat16
HIGHEST = lax.Precision.HIGHEST

D_MODEL = 1024
D_A = D_MODEL // 2
DH_A = 64
H_A = D_A // DH_A
MOBA_BLOCK = 256
MOBA_TOPK = 3
N_BUCKETS = 32
MAX_DISTANCE = 1024
D_B = D_MODEL // 2
DK = 128
DV = 128
H_B = D_B // DV
CONV_W = 4
CONV_DIM = 2 * H_B * DK + H_B * DV
DELTA_CHUNK = 64
PAGE_SIZE = 128
EPS = 1e-6

LANES = 128
SUBLANES = 8
NEG = -1e30
LOG2E = math.log2(math.e)
TOKEN_TILE = MOBA_BLOCK
SAMPLE_PAD = SUBLANES
NEAR_TILES = MAX_DISTANCE // MOBA_BLOCK + 1
MOBA_HEAD_GROUP = 8
W_MAIN = 4 * D_A + CONV_DIM + H_B * DV
W_BA = W_MAIN
W_GATES = W_MAIN + LANES
W_TOTAL = W_GATES + 2 * D_MODEL
VMEM_LIMIT = 48 * 1024 * 1024


def _cparams(*sem):
    return pltpu.CompilerParams(dimension_semantics=sem, vmem_limit_bytes=VMEM_LIMIT)


def _dot(a, b):
    return jnp.dot(a.astype(bf16), b.astype(bf16), preferred_element_type=f32)


def _dot_nt(a, b):
    return lax.dot_general(a.astype(bf16), b.astype(bf16), (((1,), (1,)), ((), ())), preferred_element_type=f32)


def _dot_tn(a, b):
    return lax.dot_general(a.astype(bf16), b.astype(bf16), (((0,), (0,)), ((), ())), preferred_element_type=f32)


def _bmm(a, b):
    return lax.dot_general(a.astype(bf16), b.astype(bf16), (((2,), (1,)), ((0,), (0,))), preferred_element_type=f32)


def _bmm_nt(a, b):
    return lax.dot_general(a.astype(bf16), b.astype(bf16), (((2,), (2,)), ((0,), (0,))), preferred_element_type=f32)


def _bmm_tn(a, b):
    return lax.dot_general(a.astype(bf16), b.astype(bf16), (((1,), (1,)), ((0,), (0,))), preferred_element_type=f32)


def _silu(x):
    return x * jax.nn.sigmoid(x)


def _softplus(x):
    return jnp.maximum(x, 0.0) + jnp.log(1.0 + jnp.exp(-jnp.abs(x)))


def _div_pow2(x, n):
    return lax.shift_right_logical(x, int(math.log2(n)))


def _rel_bucket(dist):
    n = jnp.maximum(dist, 0)
    max_exact = N_BUCKETS // 2
    nf = jnp.maximum(n, 1).astype(f32)
    large = max_exact + (jnp.log(nf / max_exact) / math.log(MAX_DISTANCE / max_exact)
                         * (N_BUCKETS - max_exact)).astype(jnp.int32)
    large = jnp.minimum(large, N_BUCKETS - 1)
    return jnp.where(n < max_exact, n, large)


def _bias_prompt_kernel(rb_ref, o_ref):
    d = pl.program_id(0)
    h = pl.program_id(1)
    r = lax.broadcasted_iota(jnp.int32, (MOBA_BLOCK, MOBA_BLOCK), 0)
    c = lax.broadcasted_iota(jnp.int32, (MOBA_BLOCK, MOBA_BLOCK), 1)
    dist = d * MOBA_BLOCK + c - r
    bucket = _rel_bucket(dist)
    acc = jnp.zeros((MOBA_BLOCK, MOBA_BLOCK), f32)
    for b in range(N_BUCKETS):
        acc = jnp.where(bucket == b, rb_ref[b, h], acc)
    o_ref[...] = jnp.where(dist >= 0, acc * LOG2E, NEG)


def _bias_prompt(rel_bias):
    return pl.pallas_call(
        _bias_prompt_kernel,
        out_shape=jax.ShapeDtypeStruct((NEAR_TILES, H_A, MOBA_BLOCK, MOBA_BLOCK), f32),
        grid=(NEAR_TILES, H_A),
        in_specs=[pl.BlockSpec(memory_space=pltpu.SMEM)],
        out_specs=pl.BlockSpec((None, None, MOBA_BLOCK, MOBA_BLOCK), lambda d, h: (d, h, 0, 0)),
        compiler_params=_cparams("arbitrary", "arbitrary"),
        name="bias_prompt",
    )(rel_bias)


def _bias_sample_kernel(rb_ref, o_ref, far_ref, *, past_len, dec_t):
    rows = H_A * SAMPLE_PAD
    width = past_len + LANES
    row = lax.broadcasted_iota(jnp.int32, (rows, width), 0)
    lane = lax.broadcasted_iota(jnp.int32, (rows, width), 1)
    rowh = _div_pow2(row, SAMPLE_PAD)
    tq = jnp.minimum(row - rowh * SAMPLE_PAD, dec_t - 1)
    tn = lane - past_len
    is_new = lane >= past_len
    dist = jnp.where(is_new, tq - tn, past_len + tq - lane)
    valid = jnp.logical_or(jnp.logical_not(is_new), jnp.logical_and(tn <= tq, tn < dec_t))
    bucket = _rel_bucket(dist)
    far_bucket = _rel_bucket(jnp.full((H_A, LANES), MAX_DISTANCE, jnp.int32))
    far_row = lax.broadcasted_iota(jnp.int32, (H_A, LANES), 0)

    def body(b, carry):
        acc, far = carry
        for h in range(H_A):
            v = rb_ref[b, h]
            acc = jnp.where(jnp.logical_and(bucket == b, rowh == h), v, acc)
            far = jnp.where(jnp.logical_and(far_bucket == b, far_row == h), v, far)
        return acc, far

    acc, far = lax.fori_loop(0, N_BUCKETS, body, (jnp.zeros((rows, width), f32), jnp.zeros((H_A, LANES), f32)))
    o_ref[...] = jnp.where(valid, acc, NEG)
    far_ref[...] = far * LOG2E


def _bias_sample(rel_bias, past_len, dec_t):
    return pl.pallas_call(
        functools.partial(_bias_sample_kernel, past_len=past_len, dec_t=dec_t),
        out_shape=(jax.ShapeDtypeStruct((H_A * SAMPLE_PAD, past_len + LANES), f32),
                   jax.ShapeDtypeStruct((H_A, LANES), f32)),
        in_specs=[pl.BlockSpec(memory_space=pltpu.SMEM)],
        name="bias_sample",
    )(rel_bias)


def _mod_kernel(c_ref, w_ref, b_ref, o_ref):
    o_ref[...] = _dot(_silu(c_ref[...]), w_ref[...]) + b_ref[...]


def _modulation(c_all, w_ada, b_ada):
    depth = w_ada.shape[0]
    rows = c_all.shape[0]
    return pl.pallas_call(
        _mod_kernel,
        out_shape=jax.ShapeDtypeStruct((depth, rows, 3 * D_MODEL), f32),
        grid=(depth, 3),
        in_specs=[pl.BlockSpec((rows, D_MODEL), lambda l, j: (0, 0)),
                  pl.BlockSpec((None, D_MODEL, D_MODEL), lambda l, j: (l, 0, j)),
                  pl.BlockSpec((None, 1, D_MODEL), lambda l, j: (l, 0, j))],
        out_specs=pl.BlockSpec((None, rows, D_MODEL), lambda l, j: (l, 0, j)),
        compiler_params=_cparams("arbitrary", "arbitrary"),
        name="modulation",
    )(c_all, w_ada, b_ada.reshape(depth, 1, 3 * D_MODEL))


def _inproj_kernel(x_ref, shift_ref, scale_ref, gpre_ref, w_ref, *outs, prompt):
    x = x_ref[...]
    h = x * lax.rsqrt(jnp.mean(x * x, axis=-1, keepdims=True) + EPS) * gpre_ref[...]
    hb = (h * (1.0 + scale_ref[...]) + shift_ref[...]).astype(bf16)

    def proj(lo, hi):
        return jnp.dot(hb, w_ref[:, lo:hi], preferred_element_type=f32)

    q = proj(0, D_A)
    k = proj(D_A, 2 * D_A)
    v = proj(2 * D_A, 3 * D_A)
    if prompt:
        k_ref, v_ref, kb_ref, qt_ref, vt_ref, km_ref, za_ref, xb_ref, zb_ref, ba_ref, ga_ref, gb_ref = outs
        vt = v.T
        k_ref[...] = k.T
        v_ref[...] = vt
        kb_ref[...] = k.astype(bf16)
        qt_ref[...] = q.T
        vt_ref[...] = vt.astype(bf16)
        km_ref[...] = jnp.mean(k, axis=0, keepdims=True)
    else:
        q_ref, k_ref, v_ref, za_ref, xb_ref, zb_ref, ba_ref, ga_ref, gb_ref = outs
        q_ref[...] = q
        k_ref[...] = k
        v_ref[...] = v
    za_ref[...] = proj(3 * D_A, 4 * D_A).astype(za_ref.dtype)
    xb_ref[...] = proj(4 * D_A, 4 * D_A + CONV_DIM)
    zb_ref[...] = proj(4 * D_A + CONV_DIM, W_MAIN).astype(zb_ref.dtype)
    ba_ref[...] = proj(W_BA, W_BA + LANES)
    ga_ref[...] = proj(W_GATES, W_GATES + D_MODEL).astype(ga_ref.dtype)
    gb_ref[...] = proj(W_GATES + D_MODEL, W_TOTAL).astype(gb_ref.dtype)


def _inproj(x, shift, scale, g_pre, w_cat, *, prompt, tiles_per_seq=None):
    n = x.shape[0]
    tm = TOKEN_TILE
    nt = n // tm
    row = lambda i: (i, 0)
    if prompt:
        mod_spec = pl.BlockSpec((None, 1, D_MODEL), lambda i: (i // tiles_per_seq, 0, 0))
    else:
        mod_spec = pl.BlockSpec((tm, D_MODEL), row)
    tok = lambda w, dt=f32: (jax.ShapeDtypeStruct((n, w), dt), pl.BlockSpec((tm, w), row))
    gdt = bf16 if prompt else f32
    common = [tok(D_A, gdt), tok(CONV_DIM), tok(D_B, gdt), tok(LANES), tok(D_MODEL, gdt), tok(D_MODEL, gdt)]
    if prompt:
        seq_t = (jax.ShapeDtypeStruct((nt // tiles_per_seq, D_A, tiles_per_seq * tm), f32),
                 pl.BlockSpec((None, D_A, tm), lambda i: (i // tiles_per_seq, 0, i % tiles_per_seq)))
        outs = [seq_t, seq_t, tok(D_A, bf16),
                (jax.ShapeDtypeStruct((nt, D_A, tm), f32), pl.BlockSpec((None, D_A, tm), lambda i: (i, 0, 0))),
                (jax.ShapeDtypeStruct((nt, D_A, tm), bf16), pl.BlockSpec((None, D_A, tm), lambda i: (i, 0, 0))),
                (jax.ShapeDtypeStruct((nt, 1, D_A), f32), pl.BlockSpec((None, 1, D_A), lambda i: (i, 0, 0)))] + common
    else:
        outs = [tok(D_A), tok(D_A), tok(D_A)] + common
    return pl.pallas_call(
        functools.partial(_inproj_kernel, prompt=prompt),
        out_shape=tuple(o[0] for o in outs),
        grid=(nt,),
        in_specs=[pl.BlockSpec((tm, D_MODEL), row), mod_spec, mod_spec,
                  pl.BlockSpec((1, D_MODEL), lambda i: (0, 0)),
                  pl.BlockSpec((D_MODEL, W_TOTAL), lambda i: (0, 0), pipeline_mode=pl.Buffered(1))],
        out_specs=tuple(o[1] for o in outs),
        compiler_params=_cparams("arbitrary"),
        name="inproj_prompt" if prompt else "inproj_sample",
    )(x, shift, scale, g_pre.reshape(1, D_MODEL), w_cat)


def _moba_prompt_kernel(far_ref, qt_ref, kb_ref, vt_ref, km_ref, bias_ref, o_ref, sel_scr, far_scr, qm_scr, *, hg):
    grp = pl.program_id(1)
    i = pl.program_id(2)
    nb = km_ref.shape[0]
    pair_w = 2 * DH_A
    blk = lax.broadcasted_iota(jnp.int32, (nb, MOBA_BLOCK), 0)
    blkf = blk.astype(f32)
    feat = lax.broadcasted_iota(jnp.int32, (pair_w, MOBA_BLOCK), 0)
    for hh in range(hg):
        pair, lo = hh // 2, (hh % 2) * DH_A
        qt = qt_ref[pair * pair_w:(pair + 1) * pair_w, :]
        qm = jnp.where(jnp.logical_and(feat >= lo, feat < lo + DH_A), qt, 0.0)
        gate = jnp.dot(km_ref[:, pair * pair_w:(pair + 1) * pair_w], qm, precision=HIGHEST,
                       preferred_element_type=f32)
        g = jnp.where(blk < i, gate, -jnp.inf)
        chosen = blk == i
        for _ in range(MOBA_TOPK):
            m = jnp.max(g, axis=0, keepdims=True)
            first = jnp.min(jnp.where(g == m, blkf, float(nb)), axis=0, keepdims=True)
            hit = jnp.logical_and(blkf == first, m > -jnp.inf)
            chosen = jnp.logical_or(chosen, hit)
            g = jnp.where(hit, -jnp.inf, g)
        sel_scr[hh] = chosen.astype(f32)
        far_scr[hh] = jnp.full((1, MOBA_BLOCK), far_ref[grp * hg + hh], f32)
        qm_scr[hh] = (qm * (DH_A ** -0.5 * LOG2E)).astype(bf16)

    def step(j, carry, near):
        m, l, acc = carry
        kj = kb_ref[j]
        kj = jnp.stack([kj[:, (hh // 2) * pair_w:(hh // 2 + 1) * pair_w] for hh in range(hg)])
        vj = vt_ref[j].reshape(hg, DH_A, MOBA_BLOCK)
        s = _bmm(kj, qm_scr[...])
        if near:
            s = bias_ref[i - j] + s
            shift = 0.0
        else:
            shift = far_scr[...]
        ok = sel_scr[:, pl.ds(j, 1), :] > 0.5
        m_new = jnp.where(ok, jnp.maximum(m, jnp.max(s, axis=1, keepdims=True) + shift), m)
        alpha = jnp.exp2(m - m_new)
        p = jnp.exp2(s - jnp.where(ok, m_new - shift, -NEG))
        l = alpha * l + jnp.sum(p, axis=1, keepdims=True)
        acc = alpha * acc + _bmm(vj, p)
        return m_new, l, acc

    init = (jnp.full((hg, 1, MOBA_BLOCK), NEG, f32), jnp.zeros((hg, 1, MOBA_BLOCK), f32),
            jnp.zeros((hg, DH_A, MOBA_BLOCK), f32))
    n_far = jnp.maximum(i - (NEAR_TILES - 1), 0)
    carry = lax.fori_loop(0, n_far, lambda j, c: step(j, c, False), init)
    _, l, acc = lax.fori_loop(n_far, i + 1, lambda j, c: step(j, c, True), carry)
    o_ref[...] = (acc / l).reshape(hg * DH_A, MOBA_BLOCK)


def _moba_prompt(qt, kb, vt, kmean, bias, far):
    bsz, nb = kb.shape[0], kb.shape[1]
    hg = MOBA_HEAD_GROUP
    gw = hg * DH_A
    return pl.pallas_call(
        functools.partial(_moba_prompt_kernel, hg=hg),
        out_shape=jax.ShapeDtypeStruct((bsz * nb, D_A, MOBA_BLOCK), f32),
        grid=(bsz, H_A // hg, nb),
        in_specs=[pl.BlockSpec(memory_space=pltpu.SMEM),
                  pl.BlockSpec((None, gw, MOBA_BLOCK), lambda b, g, i: (b * nb + i, g, 0)),
                  pl.BlockSpec((None, nb, MOBA_BLOCK, gw), lambda b, g, i: (b, 0, 0, g)),
                  pl.BlockSpec((None, nb, gw, MOBA_BLOCK), lambda b, g, i: (b, 0, g, 0)),
                  pl.BlockSpec((None, nb, gw), lambda b, g, i: (b, 0, g)),
                  pl.BlockSpec((NEAR_TILES, hg, MOBA_BLOCK, MOBA_BLOCK), lambda b, g, i: (0, g, 0, 0))],
        out_specs=pl.BlockSpec((None, gw, MOBA_BLOCK), lambda b, g, i: (b * nb + i, g, 0)),
        scratch_shapes=[pltpu.VMEM((hg, nb, MOBA_BLOCK), f32), pltpu.VMEM((hg, 1, MOBA_BLOCK), f32),
                        pltpu.VMEM((hg, 2 * DH_A, MOBA_BLOCK), bf16)],
        compiler_params=_cparams("arbitrary", "arbitrary", "arbitrary"),
        name="moba_prompt",
    )(far, qt, kb, vt, kmean, bias)


def _moba_sample_kernel(pt_ref, q_ref, kn_ref, vn_ref, bias_ref, *refs, n_pages):
    del pt_ref
    kp = refs[:n_pages]
    vp = refs[n_pages:2 * n_pages]
    o_ref, e_scr, en_scr, l_scr = refs[2 * n_pages:]
    rows = H_A * SAMPLE_PAD
    ppb = MOBA_BLOCK // PAGE_SIZE
    n_blocks = n_pages // ppb
    past = n_pages * PAGE_SIZE
    pad_rows = LANES - SAMPLE_PAD
    step = pl.program_id(0)
    slot = lax.rem(step, 2)
    rowh = _div_pow2(lax.broadcasted_iota(jnp.int32, (rows, D_A), 0), SAMPLE_PAD)
    laneh = _div_pow2(lax.broadcasted_iota(jnp.int32, (rows, D_A), 1), DH_A)
    own = rowh == laneh

    @pl.when(step == 0)
    def _():
        e_scr[1] = jnp.zeros((rows, past), bf16)
        en_scr[1] = jnp.zeros((rows, LANES), bf16)
        l_scr[1] = jnp.ones((rows, LANES), f32)

    vn = jnp.concatenate([vn_ref[...], jnp.zeros((pad_rows, D_A), f32)], axis=0)
    vt = jnp.concatenate([vp[p][...].astype(bf16) for p in range(n_pages)], axis=1)
    acc = lax.dot_general(e_scr[1 - slot], vt, (((1,), (1,)), ((), ())), preferred_element_type=f32)
    acc = acc + jnp.dot(en_scr[1 - slot], vn.astype(bf16), preferred_element_type=f32)
    o = jnp.where(own, acc / l_scr[1 - slot][:, 0:1], 0.0)
    out = o[0:SAMPLE_PAD]
    for h in range(1, H_A):
        out = out + o[h * SAMPLE_PAD:(h + 1) * SAMPLE_PAD]
    o_ref[...] = out

    qe = jnp.where(own, jnp.concatenate([q_ref[...]] * H_A, axis=0), 0.0)
    qeb = (qe * DH_A ** -0.5).astype(bf16)

    lane_f = lax.broadcasted_iota(jnp.int32, (D_A, LANES), 1)
    kmean = jnp.zeros((D_A, LANES), f32)
    for j in range(n_blocks):
        ksum = kp[j * ppb][...]
        for p in range(1, ppb):
            ksum = ksum + kp[j * ppb + p][...]
        kmean = jnp.where(lane_f == j, jnp.sum(ksum, axis=1, keepdims=True) / MOBA_BLOCK, kmean)
    gate = jnp.dot(qe, kmean, precision=HIGHEST, preferred_element_type=f32)
    lane_r = lax.broadcasted_iota(jnp.int32, (rows, LANES), 1)
    lane_rf = lane_r.astype(f32)
    g = jnp.where(lane_r < n_blocks, gate, -jnp.inf)
    chosen = jnp.zeros((rows, LANES), jnp.bool_)
    for _ in range(MOBA_TOPK):
        m = jnp.max(g, axis=1, keepdims=True)
        first = jnp.min(jnp.where(g == m, lane_rf, float(LANES)), axis=1, keepdims=True)
        hit = lane_rf == first
        chosen = jnp.logical_or(chosen, hit)
        g = jnp.where(hit, -jnp.inf, g)
    sel = jnp.where(chosen, 0.0, NEG)

    kn = jnp.concatenate([kn_ref[...], jnp.zeros((pad_rows, D_A), f32)], axis=0)
    kt = jnp.concatenate([kp[p][...].astype(bf16) for p in range(n_pages)], axis=1)
    sel_past = jnp.concatenate([jnp.broadcast_to(sel[:, j:j + 1], (rows, MOBA_BLOCK)) for j in range(n_blocks)],
                               axis=1)
    s_past = bias_ref[:, :past] + sel_past + jnp.dot(qeb, kt, preferred_element_type=f32)
    s_new = _dot_nt(qeb, kn) + bias_ref[:, past:past + LANES]
    m = jnp.maximum(jnp.max(s_past, axis=-1, keepdims=True), jnp.max(s_new, axis=-1, keepdims=True))
    e_past = jnp.exp(s_past - m)
    e_new = jnp.exp(s_new - m)
    l = jnp.sum(e_past, axis=-1, keepdims=True) + jnp.sum(e_new, axis=-1, keepdims=True)
    e_scr[slot] = e_past.astype(bf16)
    en_scr[slot] = e_new.astype(bf16)
    l_scr[slot] = jnp.broadcast_to(l, (rows, LANES))


def _moba_sample(q, kn, vn, cache_k, cache_v, page_table, bias, layer):
    nseq, n_pages = page_table.shape
    assert (n_pages * PAGE_SIZE) % MOBA_BLOCK == 0 and (n_pages * PAGE_SIZE) // MOBA_BLOCK >= MOBA_TOPK
    rows = H_A * SAMPLE_PAD
    cur = lambda b: jnp.minimum(b, nseq - 1)
    prev = lambda b: jnp.maximum(b - 1, 0)
    seq = lambda which: pl.BlockSpec((None, SAMPLE_PAD, D_A), lambda b, pt: (which(b), 0, 0))
    page = lambda which, p: pl.BlockSpec((None, None, D_A, PAGE_SIZE),
                                         lambda b, pt: (layer, pt[which(b), p], 0, 0))
    grid_spec = pltpu.PrefetchScalarGridSpec(
        num_scalar_prefetch=1,
        grid=(nseq + 1,),
        in_specs=[seq(cur), seq(cur), seq(prev), pl.BlockSpec(bias.shape, lambda b, pt: (0, 0))]
        + [page(cur, p) for p in range(n_pages)] + [page(prev, p) for p in range(n_pages)],
        out_specs=seq(prev),
        scratch_shapes=[pltpu.VMEM((2, rows, n_pages * PAGE_SIZE), bf16), pltpu.VMEM((2, rows, LANES), bf16),
                        pltpu.VMEM((2, rows, LANES), f32)],
    )
    return pl.pallas_call(
        functools.partial(_moba_sample_kernel, n_pages=n_pages),
        out_shape=jax.ShapeDtypeStruct((nseq, SAMPLE_PAD, D_A), f32),
        grid_spec=grid_spec,
        compiler_params=_cparams("arbitrary"),
        name="moba_sample",
    )(page_table, q, kn, vn, bias, *([cache_k] * n_pages), *([cache_v] * n_pages))


def _gdn_kernel(x_ref, ba_ref, z_ref, conv0_ref, s0_ref, cw_ref, avec_ref, dtvec_ref, gon_ref,
                o_ref, s_ref, xs_scr, *, chunk, n_valid):
    t = pl.program_id(1)
    nb, tt = x_ref.shape[0], x_ref.shape[1]

    @pl.when(t == 0)
    def _():
        s_ref[...] = s0_ref[...]
        xs_scr[:, 0:SUBLANES, :] = conv0_ref[...]

    xs_scr[:, SUBLANES:SUBLANES + tt, :] = x_ref[...]

    ci = lax.broadcasted_iota(jnp.int32, (chunk, chunk), 0)
    cj = lax.broadcasted_iota(jnp.int32, (chunk, chunk), 1)
    causal = ci >= cj
    strict = ci > cj
    tri = causal.astype(f32)
    eye = (ci == cj).astype(f32)
    cw = cw_ref[...]
    neg_a = -jnp.exp(avec_ref[...])
    merges = []
    for b in range(int(math.log2(chunk))):
        same_pair = lax.shift_right_logical(ci, b + 1) == lax.shift_right_logical(cj, b + 1)
        row_hi = jnp.bitwise_and(lax.shift_right_logical(ci, b), 1) == 1
        col_lo = jnp.bitwise_and(lax.shift_right_logical(cj, b), 1) == 0
        merges.append(jnp.logical_and(same_pair, jnp.logical_and(row_hi, col_lo)))

    nc = tt // chunk
    qs, ks, vs, zs, bcols, gcols = [], [], [], [], [], []
    base = SUBLANES - (CONV_W - 1)
    for n in range(nb):
        u = xs_scr[n, base:base + tt, :] * cw[0:1]
        for j in range(1, CONV_W):
            u = u + xs_scr[n, base + j:base + j + tt, :] * cw[j:j + 1]
        u = _silu(u)
        ba = ba_ref[n]
        beta = jax.nn.sigmoid(ba)
        g = neg_a * _softplus(ba + dtvec_ref[...])
        if n_valid < tt:
            keep = lax.broadcasted_iota(jnp.int32, (tt, LANES), 0) < n_valid
            beta = jnp.where(keep, beta, 0.0)
            g = jnp.where(keep, g, 0.0)
        for c in range(nc):
            rows = slice(c * chunk, (c + 1) * chunk)
            gcum = jnp.dot(tri, g[rows], precision=HIGHEST, preferred_element_type=f32)
            for h in range(H_B):
                qs.append(u[rows, h * DK:(h + 1) * DK])
                ks.append(u[rows, H_B * DK + h * DK:H_B * DK + (h + 1) * DK])
                vs.append(u[rows, 2 * H_B * DK + h * DV:2 * H_B * DK + (h + 1) * DV])
                zs.append(z_ref[n, rows, h * DV:(h + 1) * DV])
                bcols.append(beta[rows, h:h + 1])
                gcols.append(gcum[:, H_B + h:H_B + h + 1])
    q, k, v, z = jnp.stack(qs), jnp.stack(ks), jnp.stack(vs), jnp.stack(zs)
    bcol, gcol = jnp.stack(bcols), jnp.stack(gcols)
    q = q * lax.rsqrt(jnp.sum(q * q, axis=-1, keepdims=True) + EPS) * DK ** -0.5
    k = k * lax.rsqrt(jnp.sum(k * k, axis=-1, keepdims=True) + EPS)
    grow = jnp.sum(eye[None] * gcol, axis=1, keepdims=True)
    decay = jnp.exp(jnp.where(causal[None], gcol - grow, NEG))
    kb = k * bcol
    lower = jnp.where(strict[None], _bmm_nt(kb, k) * decay, 0.0)
    inv = eye[None] - jnp.where(merges[0][None], lower, 0.0)
    for merge in merges[1:]:
        inv = inv - _bmm(_bmm(inv, jnp.where(merge[None], lower, 0.0)), inv)
    sol = _bmm(inv, jnp.concatenate([v * bcol, kb * jnp.exp(gcol)], axis=2))
    uu, ww = sol[:, :, :DV], sol[:, :, DV:]
    qk = jnp.where(causal[None], _bmm_nt(q, k) * decay, 0.0)
    qe = q * jnp.exp(gcol)
    g_last = gcol[:, chunk - 1:chunk, :]
    ke = k * jnp.exp(g_last - gcol)
    carry = jnp.exp(g_last)

    def at_chunk(a, c):
        return a.reshape((nb, nc, H_B) + a.shape[1:])[:, c].reshape((nb * H_B,) + a.shape[1:])

    state = s_ref[...].reshape(nb * H_B, DK, DV)
    for c in range(nc):
        v_new = at_chunk(uu, c) - _bmm(at_chunk(ww, c), state)
        o = _bmm(at_chunk(qe, c), state) + _bmm(at_chunk(qk, c), v_new)
        state = state * at_chunk(carry, c) + _bmm_tn(at_chunk(ke, c), v_new)
        o = o * lax.rsqrt(jnp.mean(o * o, axis=-1, keepdims=True) + EPS) * gon_ref[...]
        o = o * _silu(at_chunk(z, c))
        for n in range(nb):
            for h in range(H_B):
                o_ref[n, c * chunk:(c + 1) * chunk, h * DV:(h + 1) * DV] = o[n * H_B + h].astype(o_ref.dtype)
    s_ref[...] = state.reshape(nb, H_B, DK, DV)

    xs_scr[:, 0:SUBLANES, :] = xs_scr[:, tt:tt + SUBLANES, :]


def _gdn(xb, ba, zb, conv0, s0, layer, conv_w, a_log, dt_bias, g_onorm, *, nb, tt, chunk, n_valid):
    nseq, t_len = xb.shape[0], xb.shape[1]
    lane_vec = lambda v: jnp.zeros((1, LANES), f32).at[0, H_B:2 * H_B].set(v)
    tok = lambda w: pl.BlockSpec((nb, tt, w), lambda g, t: (g, t, 0))
    const = lambda shape: pl.BlockSpec(shape, lambda g, t: (0,) * len(shape))
    return pl.pallas_call(
        functools.partial(_gdn_kernel, chunk=chunk, n_valid=n_valid),
        out_shape=(jax.ShapeDtypeStruct((nseq, t_len, D_B), zb.dtype),
                   jax.ShapeDtypeStruct((nseq, H_B, DK, DV), f32)),
        grid=(nseq // nb, t_len // tt),
        in_specs=[tok(CONV_DIM), tok(LANES), tok(D_B),
                  pl.BlockSpec((None, nb, SUBLANES, CONV_DIM), lambda g, t: (layer, g, 0, 0)),
                  pl.BlockSpec((None, nb, H_B, DK, DV), lambda g, t: (layer, g, 0, 0, 0)),
                  const((CONV_W, CONV_DIM)), const((1, LANES)), const((1, LANES)), const((1, DV))],
        out_specs=(tok(D_B), pl.BlockSpec((nb, H_B, DK, DV), lambda g, t: (g, 0, 0, 0))),
        scratch_shapes=[pltpu.VMEM((nb, tt + SUBLANES, CONV_DIM), f32)],
        compiler_params=_cparams("arbitrary", "arbitrary"),
        name="gdn_%d" % tt,
    )(xb, ba, zb, conv0, s0, conv_w, lane_vec(a_log), lane_vec(dt_bias), g_onorm.reshape(1, DV))


def _outproj_kernel(oa_ref, za_ref, ob_ref, ga_ref, gb_ref, x_ref, gate_ref, wpa_ref, wpb_ref, wout_ref,
                    gpost_ref, y_ref, *, prompt):
    oa = oa_ref[...].T if prompt else oa_ref[...]
    oa = oa * _silu(za_ref[...])
    merged = (jax.nn.sigmoid(ga_ref[...]) * _dot(oa, wpa_ref[...])
              + jax.nn.sigmoid(gb_ref[...]) * _dot(ob_ref[...], wpb_ref[...]))
    y = _dot(merged, wout_ref[...])
    y = y * lax.rsqrt(jnp.mean(y * y, axis=-1, keepdims=True) + EPS) * gpost_ref[...]
    y_ref[...] = x_ref[...] + gate_ref[...] * y


def _outproj(oa, za, ob, ga, gb, x, gate, w_pa, w_pb, w_out, g_post, *, prompt, tiles_per_seq=None):
    n = x.shape[0]
    tm = TOKEN_TILE
    row = lambda i: (i, 0)
    tok = lambda w: pl.BlockSpec((tm, w), row)
    const = lambda shape: pl.BlockSpec(shape, lambda i: (0, 0))
    if prompt:
        oa_spec = pl.BlockSpec((None, D_A, tm), lambda i: (i, 0, 0))
        gate_spec = pl.BlockSpec((None, 1, D_MODEL), lambda i: (i // tiles_per_seq, 0, 0))
    else:
        oa_spec = tok(D_A)
        gate_spec = tok(D_MODEL)
    return pl.pallas_call(
        functools.partial(_outproj_kernel, prompt=prompt),
        out_shape=jax.ShapeDtypeStruct((n, D_MODEL), f32),
        grid=(n // tm,),
        in_specs=[oa_spec, tok(D_A), tok(D_B), tok(D_MODEL), tok(D_MODEL), tok(D_MODEL), gate_spec,
                  const((D_A, D_MODEL)), const((D_B, D_MODEL)), const((D_MODEL, D_MODEL)), const((1, D_MODEL))],
        out_specs=tok(D_MODEL),
        compiler_params=_cparams("arbitrary"),
        name="outproj_prompt" if prompt else "outproj_sample",
    )(oa, za, ob, ga, gb, x, gate, w_pa, w_pb, w_out, g_post.reshape(1, D_MODEL))


def _pack_w_in(w_in):
    depth = w_in.shape[0]
    n_ba = 2 * H_B
    ba = jnp.zeros((depth, D_MODEL, LANES), w_in.dtype).at[:, :, :n_ba].set(w_in[:, :, W_MAIN:W_MAIN + n_ba])
    return jnp.concatenate([w_in[:, :, :W_MAIN], ba, w_in[:, :, W_MAIN + n_ba:]], axis=-1).astype(bf16)


def _prompt_layer(x, mod, lw, bias, far, zero_conv, zero_state):
    bsz, t_len, _ = x.shape
    n = bsz * t_len
    nb = t_len // MOBA_BLOCK
    g_pre, g_post, w_cat, conv_w, a_log, dt_bias, g_onorm, w_pa, w_pb, w_out = lw
    shift, scale, gate = (mod[:, None, j * D_MODEL:(j + 1) * D_MODEL] for j in range(3))
    xf = x.reshape(n, D_MODEL)
    k, v, kb, qt, vt, km, za, xb, zb, ba, ga, gb = _inproj(xf, shift, scale, g_pre, w_cat, prompt=True,
                                                             tiles_per_seq=nb)
    ot = _moba_prompt(qt, kb.reshape(bsz, nb, MOBA_BLOCK, D_A), vt.reshape(bsz, nb, D_A, MOBA_BLOCK),
                      km.reshape(bsz, nb, D_A), bias, far)
    xb3 = xb.reshape(bsz, t_len, CONV_DIM)
    ob, s_new = _gdn(xb3, ba.reshape(bsz, t_len, LANES), zb.reshape(bsz, t_len, D_B), zero_conv, zero_state, 0,
                     conv_w, a_log, dt_bias, g_onorm, nb=math.gcd(bsz, 2), tt=TOKEN_TILE,
                     chunk=math.gcd(t_len, DELTA_CHUNK),
                     n_valid=TOKEN_TILE)
    y = _outproj(ot, za, ob.reshape(n, D_B), ga, gb, xf, gate, w_pa, w_pb, w_out, g_post, prompt=True,
                 tiles_per_seq=nb)
    heads = lambda a: jnp.transpose(a.reshape(bsz, H_A, DH_A, t_len), (0, 3, 1, 2))
    return y.reshape(bsz, t_len, D_MODEL), heads(k), heads(v), s_new, xb3[:, t_len - (CONV_W - 1):]


def _sample_layer(x, mod, lw, bias, cache_k, cache_v, page_table, state_delta, conv_pad, layer, dec_t):
    nseq = x.shape[0]
    n = nseq * SAMPLE_PAD
    g_pre, g_post, w_cat, conv_w, a_log, dt_bias, g_onorm, w_pa, w_pb, w_out = lw
    per_tok = jnp.repeat(mod, SAMPLE_PAD, axis=0)
    shift, scale, gate = (per_tok[:, j * D_MODEL:(j + 1) * D_MODEL] for j in range(3))
    xf = x.reshape(n, D_MODEL)
    q, k, v, za, xb, zb, ba, ga, gb = _inproj(xf, shift, scale, g_pre, w_cat, prompt=False)
    seq3 = lambda a: a.reshape(nseq, SAMPLE_PAD, a.shape[-1])
    oa = _moba_sample(seq3(q), seq3(k), seq3(v), cache_k, cache_v, page_table, bias, layer)
    xb3 = seq3(xb)
    ob, s_new = _gdn(xb3, seq3(ba), seq3(zb), conv_pad, state_delta, layer, conv_w, a_log, dt_bias, g_onorm,
                     nb=SUBLANES, tt=SAMPLE_PAD, chunk=SAMPLE_PAD, n_valid=dec_t)
    y = _outproj(oa.reshape(n, D_A), za, ob.reshape(n, D_B), ga, gb, xf, gate, w_pa, w_pb, w_out, g_post,
                 prompt=False)
    heads = lambda a: seq3(a)[:, :dec_t].reshape(nseq, dec_t, H_A, DH_A)
    return y.reshape(nseq, SAMPLE_PAD, D_MODEL), heads(k), heads(v), s_new, xb3[:, dec_t - (CONV_W - 1):dec_t]


def kernel(x_prompt, x_sample, c_prompt, c_sample, cache_k, cache_v, state_delta, state_conv, page_table, rel_bias, w_ada, b_ada, g_pre, g_post, w_in, conv_w, a_log, dt_bias, g_onorm, w_pa, w_pb, w_out):
    bsz, seq, _ = x_prompt.shape
    dec_b, dec_t, _ = x_sample.shape
    depth = w_in.shape[0]
    n_pages = page_table.shape[1]
    past_len = n_pages * PAGE_SIZE
    assert seq % MOBA_BLOCK == 0 and seq >= CONV_W - 1 and CONV_W - 1 <= dec_t <= SAMPLE_PAD
    assert dec_b % SUBLANES == 0

    bias_p = _bias_prompt(rel_bias)
    bias_s, far = _bias_sample(rel_bias, past_len, dec_t)
    far = far[:, 0]
    c_all = jnp.concatenate([c_prompt, c_sample], axis=0)
    c_all = jnp.pad(c_all, ((0, -c_all.shape[0] % (2 * SUBLANES)), (0, 0)))
    mod = _modulation(c_all, w_ada, b_ada)
    w_cat = _pack_w_in(w_in)
    w_pa_b, w_pb_b, w_out_b = w_pa.astype(bf16), w_pb.astype(bf16), w_out.astype(bf16)
    ck = jnp.transpose(cache_k, (0, 1, 3, 4, 2)).reshape(cache_k.shape[0], cache_k.shape[1], D_A, PAGE_SIZE)
    cv = jnp.transpose(cache_v, (0, 1, 3, 4, 2)).reshape(cache_v.shape[0], cache_v.shape[1], D_A, PAGE_SIZE)
    conv_pad = jnp.pad(state_conv, ((0, 0), (0, 0), (SUBLANES - (CONV_W - 1), 0), (0, 0)))
    zero_conv = jnp.zeros((1, bsz, SUBLANES, CONV_DIM), f32)
    zero_state = jnp.zeros((1, bsz, H_B, DK, DV), f32)

    hp = x_prompt
    hs = jnp.pad(x_sample, ((0, 0), (0, SAMPLE_PAD - dec_t), (0, 0)))
    outs = [[] for _ in range(8)]
    for l in range(depth):
        lw = (g_pre[l], g_post[l], w_cat[l], conv_w[l], a_log[l], dt_bias[l], g_onorm[l],
              w_pa_b[l], w_pb_b[l], w_out_b[l])
        hp, kp, vp, sp, cp = _prompt_layer(hp, mod[l, :bsz], lw, bias_p, far, zero_conv, zero_state)
        hs, ks, vs, ss, cs = _sample_layer(hs, mod[l, bsz:bsz + dec_b], lw, bias_s, ck, cv, page_table, state_delta,
                                           conv_pad, l, dec_t)
        for lst, val in zip(outs, (kp, vp, ks, vs, sp, ss, cp, cs)):
            lst.append(val)
    return (hp, hs[:, :dec_t]) + tuple(jnp.stack(o) for o in outs)
```

```python
import functools
import math

import jax
import jax.numpy as jnp
from jax import lax
from jax.experimental import pallas as pl
from jax.experimental.pallas import tpu as pltpu

f32 = jnp.float32
bf16 = jnp.bfloat16
HIGHEST = lax.Precision.HIGHEST

D_MODEL = 1024
D_A = D_MODEL // 2
DH_A = 64
H_A = D_A // DH_A
MOBA_BLOCK = 256
MOBA_TOPK = 3
N_BUCKETS = 32
MAX_DISTANCE = 1024
D_B = D_MODEL // 2
DK = 128
DV = 128
H_B = D_B // DV
CONV_W = 4
CONV_DIM = 2 * H_B * DK + H_B * DV
DELTA_CHUNK = 64
PAGE_SIZE = 128
EPS = 1e-6

LANES = 128
SUBLANES = 8
NEG = -1e30
LOG2E = math.log2(math.e)
TOKEN_TILE = MOBA_BLOCK
SAMPLE_PAD = SUBLANES
NEAR_TILES = MAX_DISTANCE // MOBA_BLOCK + 1
MOBA_HEAD_GROUP = 8
DEN_ROWS = 2 * SUBLANES
W_MAIN = 4 * D_A + CONV_DIM + H_B * DV
W_BA = W_MAIN
W_GATES = W_MAIN + LANES
W_TOTAL = W_GATES + 2 * D_MODEL
VMEM_LIMIT = 48 * 1024 * 1024


def _cparams(*sem):
    return pltpu.CompilerParams(dimension_semantics=sem, vmem_limit_bytes=VMEM_LIMIT)


def _dot(a, b):
    return jnp.dot(a.astype(bf16), b.astype(bf16), preferred_element_type=f32)


def _dot_nt(a, b):
    return lax.dot_general(a.astype(bf16), b.astype(bf16), (((1,), (1,)), ((), ())), preferred_element_type=f32)


def _dot_tn(a, b):
    return lax.dot_general(a.astype(bf16), b.astype(bf16), (((0,), (0,)), ((), ())), preferred_element_type=f32)


def _bmm(a, b):
    return lax.dot_general(a.astype(bf16), b.astype(bf16), (((2,), (1,)), ((0,), (0,))), preferred_element_type=f32)


def _bmm_nt(a, b):
    return lax.dot_general(a.astype(bf16), b.astype(bf16), (((2,), (2,)), ((0,), (0,))), preferred_element_type=f32)


def _bmm_tn(a, b):
    return lax.dot_general(a.astype(bf16), b.astype(bf16), (((1,), (1,)), ((0,), (0,))), preferred_element_type=f32)


def _silu(x):
    return x * jax.nn.sigmoid(x)


def _softplus(x):
    return jnp.maximum(x, 0.0) + jnp.log(1.0 + jnp.exp(-jnp.abs(x)))


def _div_pow2(x, n):
    return lax.shift_right_logical(x, int(math.log2(n)))


def _rel_bucket(dist):
    n = jnp.maximum(dist, 0)
    max_exact = N_BUCKETS // 2
    nf = jnp.maximum(n, 1).astype(f32)
    large = max_exact + (jnp.log(nf / max_exact) / math.log(MAX_DISTANCE / max_exact)
                         * (N_BUCKETS - max_exact)).astype(jnp.int32)
    large = jnp.minimum(large, N_BUCKETS - 1)
    return jnp.where(n < max_exact, n, large)


def _bias_prompt_kernel(rb_ref, o_ref):
    d = pl.program_id(0)
    h = pl.program_id(1)
    r = lax.broadcasted_iota(jnp.int32, (MOBA_BLOCK, MOBA_BLOCK), 0)
    c = lax.broadcasted_iota(jnp.int32, (MOBA_BLOCK, MOBA_BLOCK), 1)
    dist = d * MOBA_BLOCK + c - r
    bucket = _rel_bucket(dist)
    acc = jnp.zeros((MOBA_BLOCK, MOBA_BLOCK), f32)
    for b in range(N_BUCKETS):
        acc = jnp.where(bucket == b, rb_ref[b, h], acc)
    o_ref[...] = jnp.where(dist >= 0, acc * LOG2E, NEG)


def _bias_prompt(rel_bias):
    return pl.pallas_call(
        _bias_prompt_kernel,
        out_shape=jax.ShapeDtypeStruct((NEAR_TILES, H_A, MOBA_BLOCK, MOBA_BLOCK), f32),
        grid=(NEAR_TILES, H_A),
        in_specs=[pl.BlockSpec(memory_space=pltpu.SMEM)],
        out_specs=pl.BlockSpec((None, None, MOBA_BLOCK, MOBA_BLOCK), lambda d, h: (d, h, 0, 0)),
        compiler_params=_cparams("arbitrary", "arbitrary"),
        name="bias_prompt",
    )(rel_bias)


def _bias_sample_kernel(rb_ref, o_ref, far_ref, *, past_len, dec_t):
    rows = H_A * SAMPLE_PAD
    width = past_len + LANES
    row = lax.broadcasted_iota(jnp.int32, (rows, width), 0)
    lane = lax.broadcasted_iota(jnp.int32, (rows, width), 1)
    rowh = _div_pow2(row, SAMPLE_PAD)
    tq = jnp.minimum(row - rowh * SAMPLE_PAD, dec_t - 1)
    tn = lane - past_len
    is_new = lane >= past_len
    dist = jnp.where(is_new, tq - tn, past_len + tq - lane)
    valid = jnp.logical_or(jnp.logical_not(is_new), jnp.logical_and(tn <= tq, tn < dec_t))
    bucket = _rel_bucket(dist)
    far_bucket = _rel_bucket(jnp.full((H_A, LANES), MAX_DISTANCE, jnp.int32))
    far_row = lax.broadcasted_iota(jnp.int32, (H_A, LANES), 0)

    def body(b, carry):
        acc, far = carry
        for h in range(H_A):
            v = rb_ref[b, h]
            acc = jnp.where(jnp.logical_and(bucket == b, rowh == h), v, acc)
            far = jnp.where(jnp.logical_and(far_bucket == b, far_row == h), v, far)
        return acc, far

    acc, far = lax.fori_loop(0, N_BUCKETS, body, (jnp.zeros((rows, width), f32), jnp.zeros((H_A, LANES), f32)))
    o_ref[...] = jnp.where(valid, acc, NEG)
    far_ref[...] = far * LOG2E


def _bias_sample(rel_bias, past_len, dec_t):
    return pl.pallas_call(
        functools.partial(_bias_sample_kernel, past_len=past_len, dec_t=dec_t),
        out_shape=(jax.ShapeDtypeStruct((H_A * SAMPLE_PAD, past_len + LANES), f32),
                   jax.ShapeDtypeStruct((H_A, LANES), f32)),
        in_specs=[pl.BlockSpec(memory_space=pltpu.SMEM)],
        name="bias_sample",
    )(rel_bias)


def _mod_kernel(c_ref, w_ref, b_ref, o_ref):
    o_ref[...] = _dot(_silu(c_ref[...]), w_ref[...]) + b_ref[...]


def _modulation(c_all, w_ada, b_ada):
    depth = w_ada.shape[0]
    rows = c_all.shape[0]
    return pl.pallas_call(
        _mod_kernel,
        out_shape=jax.ShapeDtypeStruct((depth, rows, 3 * D_MODEL), f32),
        grid=(depth, 3),
        in_specs=[pl.BlockSpec((rows, D_MODEL), lambda l, j: (0, 0)),
                  pl.BlockSpec((None, D_MODEL, D_MODEL), lambda l, j: (l, 0, j)),
                  pl.BlockSpec((None, 1, D_MODEL), lambda l, j: (l, 0, j))],
        out_specs=pl.BlockSpec((None, rows, D_MODEL), lambda l, j: (l, 0, j)),
        compiler_params=_cparams("arbitrary", "arbitrary"),
        name="modulation",
    )(c_all, w_ada, b_ada.reshape(depth, 1, 3 * D_MODEL))


def _inproj_kernel(x_ref, shift_ref, scale_ref, gpre_ref, w_ref, *outs, prompt):
    x = x_ref[...]
    h = x * lax.rsqrt(jnp.mean(x * x, axis=-1, keepdims=True) + EPS) * gpre_ref[...]
    hb = (h * (1.0 + scale_ref[...]) + shift_ref[...]).astype(bf16)

    def proj(lo, hi):
        return jnp.dot(hb, w_ref[:, lo:hi], preferred_element_type=f32)

    q = proj(0, D_A)
    k = proj(D_A, 2 * D_A)
    v = proj(2 * D_A, 3 * D_A)
    if prompt:
        k_ref, v_ref, kb_ref, qt_ref, vt_ref, km_ref, za_ref, xb_ref, zb_ref, ba_ref, ga_ref, gb_ref = outs
        vt = v.T
        k_ref[...] = k.T
        v_ref[...] = vt
        kb_ref[...] = k.astype(bf16)
        qt_ref[...] = q.T
        vt_ref[...] = vt.astype(bf16)
        km_ref[...] = jnp.mean(k, axis=0, keepdims=True)
    else:
        q_ref, k_ref, v_ref, za_ref, xb_ref, zb_ref, ba_ref, ga_ref, gb_ref = outs
        q_ref[...] = q
        k_ref[...] = k
        v_ref[...] = v
    za_ref[...] = proj(3 * D_A, 4 * D_A).astype(za_ref.dtype)
    xb_ref[...] = proj(4 * D_A, 4 * D_A + CONV_DIM)
    zb_ref[...] = proj(4 * D_A + CONV_DIM, W_MAIN).astype(zb_ref.dtype)
    ba_ref[...] = proj(W_BA, W_BA + LANES)
    ga_ref[...] = proj(W_GATES, W_GATES + D_MODEL).astype(ga_ref.dtype)
    gb_ref[...] = proj(W_GATES + D_MODEL, W_TOTAL).astype(gb_ref.dtype)


def _inproj(x, shift, scale, g_pre, w_cat, *, prompt, tiles_per_seq=None):
    n = x.shape[0]
    tm = TOKEN_TILE
    nt = n // tm
    row = lambda i: (i, 0)
    if prompt:
        mod_spec = pl.BlockSpec((None, 1, D_MODEL), lambda i: (i // tiles_per_seq, 0, 0))
    else:
        mod_spec = pl.BlockSpec((tm, D_MODEL), row)
    tok = lambda w, dt=f32: (jax.ShapeDtypeStruct((n, w), dt), pl.BlockSpec((tm, w), row))
    gdt = bf16 if prompt else f32
    common = [tok(D_A, gdt), tok(CONV_DIM), tok(D_B, gdt), tok(LANES), tok(D_MODEL, gdt), tok(D_MODEL, gdt)]
    if prompt:
        seq_t = (jax.ShapeDtypeStruct((nt // tiles_per_seq, D_A, tiles_per_seq * tm), f32),
                 pl.BlockSpec((None, D_A, tm), lambda i: (i // tiles_per_seq, 0, i % tiles_per_seq)))
        outs = [seq_t, seq_t, tok(D_A, bf16),
                (jax.ShapeDtypeStruct((nt, D_A, tm), f32), pl.BlockSpec((None, D_A, tm), lambda i: (i, 0, 0))),
                (jax.ShapeDtypeStruct((nt, D_A, tm), bf16), pl.BlockSpec((None, D_A, tm), lambda i: (i, 0, 0))),
                (jax.ShapeDtypeStruct((nt, 1, D_A), f32), pl.BlockSpec((None, 1, D_A), lambda i: (i, 0, 0)))] + common
    else:
        outs = [tok(D_A), tok(D_A), tok(D_A)] + common
    return pl.pallas_call(
        functools.partial(_inproj_kernel, prompt=prompt),
        out_shape=tuple(o[0] for o in outs),
        grid=(nt,),
        in_specs=[pl.BlockSpec((tm, D_MODEL), row), mod_spec, mod_spec,
                  pl.BlockSpec((1, D_MODEL), lambda i: (0, 0)),
                  pl.BlockSpec((D_MODEL, W_TOTAL), lambda i: (0, 0), pipeline_mode=pl.Buffered(1))],
        out_specs=tuple(o[1] for o in outs),
        compiler_params=_cparams("arbitrary"),
        name="inproj_prompt" if prompt else "inproj_sample",
    )(x, shift, scale, g_pre.reshape(1, D_MODEL), w_cat)


def _moba_prompt_kernel(far_ref, qt_ref, kb_ref, vt_ref, km_ref, bias_ref, o_ref, sel_scr, far_scr, qm_scr, *, hg):
    grp = pl.program_id(1)
    i = pl.program_id(2)
    nb = km_ref.shape[0]
    pair_w = 2 * DH_A
    blk = lax.broadcasted_iota(jnp.int32, (nb, MOBA_BLOCK), 0)
    blkf = blk.astype(f32)
    feat = lax.broadcasted_iota(jnp.int32, (pair_w, MOBA_BLOCK), 0)
    for hh in range(hg):
        pair, lo = hh // 2, (hh % 2) * DH_A
        qt = qt_ref[pair * pair_w:(pair + 1) * pair_w, :]
        qm = jnp.where(jnp.logical_and(feat >= lo, feat < lo + DH_A), qt, 0.0)
        gate = jnp.dot(km_ref[:, pair * pair_w:(pair + 1) * pair_w], qm, precision=HIGHEST,
                       preferred_element_type=f32)
        g = jnp.where(blk < i, gate, -jnp.inf)
        chosen = blk == i
        for _ in range(MOBA_TOPK):
            m = jnp.max(g, axis=0, keepdims=True)
            first = jnp.min(jnp.where(g == m, blkf, float(nb)), axis=0, keepdims=True)
            hit = jnp.logical_and(blkf == first, m > -jnp.inf)
            chosen = jnp.logical_or(chosen, hit)
            g = jnp.where(hit, -jnp.inf, g)
        sel_scr[hh] = chosen.astype(f32)
        far_scr[hh] = jnp.full((1, MOBA_BLOCK), far_ref[grp * hg + hh], f32)
        qm_scr[hh] = (qm * (DH_A ** -0.5 * LOG2E)).astype(bf16)

    def step(j, carry, near):
        m, acc = carry
        kj = kb_ref[j]
        kj = jnp.stack([kj[:, (hh // 2) * pair_w:(hh // 2 + 1) * pair_w] for hh in range(hg)])
        vj = jnp.concatenate([vt_ref[j].reshape(hg, DH_A, MOBA_BLOCK),
                              jnp.ones((hg, DEN_ROWS, MOBA_BLOCK), bf16)], axis=1)
        s = _bmm(kj, qm_scr[...])
        if near:
            s = bias_ref[i - j] + s
            shift = 0.0
        else:
            shift = far_scr[...]
        ok = sel_scr[:, pl.ds(j, 1), :] > 0.5
        m_new = jnp.where(ok, jnp.maximum(m, jnp.max(s, axis=1, keepdims=True) + shift), m)
        alpha = jnp.exp2(m - m_new)
        p = jnp.exp2((s - jnp.where(ok, m_new - shift, -NEG)).astype(bf16))
        acc = alpha * acc + _bmm(vj, p)
        return m_new, acc

    init = (jnp.full((hg, 1, MOBA_BLOCK), NEG, f32), jnp.zeros((hg, DH_A + DEN_ROWS, MOBA_BLOCK), f32))
    def sweep(lo, hi, carry, near):
        pairs = lax.shift_right_logical(hi - lo, 1)
        carry = lax.fori_loop(0, pairs, lambda t, c: step(lo + 2 * t + 1, step(lo + 2 * t, c, near), near), carry)
        return lax.fori_loop(lo + 2 * pairs, hi, lambda j, c: step(j, c, near), carry)

    n_far = jnp.maximum(i - (NEAR_TILES - 1), 0)
    _, acc = sweep(n_far, i + 1, sweep(0, n_far, init, False), True)
    o_ref[...] = (acc[:, :DH_A, :] / acc[:, DH_A:DH_A + 1, :]).reshape(hg * DH_A, MOBA_BLOCK)


def _moba_prompt(qt, kb, vt, kmean, bias, far):
    bsz, nb = kb.shape[0], kb.shape[1]
    hg = MOBA_HEAD_GROUP
    gw = hg * DH_A
    return pl.pallas_call(
        functools.partial(_moba_prompt_kernel, hg=hg),
        out_shape=jax.ShapeDtypeStruct((bsz * nb, D_A, MOBA_BLOCK), f32),
        grid=(bsz, H_A // hg, nb),
        in_specs=[pl.BlockSpec(memory_space=pltpu.SMEM),
                  pl.BlockSpec((None, gw, MOBA_BLOCK), lambda b, g, i: (b * nb + i, g, 0)),
                  pl.BlockSpec((None, nb, MOBA_BLOCK, gw), lambda b, g, i: (b, 0, 0, g)),
                  pl.BlockSpec((None, nb, gw, MOBA_BLOCK), lambda b, g, i: (b, 0, g, 0)),
                  pl.BlockSpec((None, nb, gw), lambda b, g, i: (b, 0, g)),
                  pl.BlockSpec((NEAR_TILES, hg, MOBA_BLOCK, MOBA_BLOCK), lambda b, g, i: (0, g, 0, 0))],
        out_specs=pl.BlockSpec((None, gw, MOBA_BLOCK), lambda b, g, i: (b * nb + i, g, 0)),
        scratch_shapes=[pltpu.VMEM((hg, nb, MOBA_BLOCK), f32), pltpu.VMEM((hg, 1, MOBA_BLOCK), f32),
                        pltpu.VMEM((hg, 2 * DH_A, MOBA_BLOCK), bf16)],
        compiler_params=_cparams("arbitrary", "arbitrary", "arbitrary"),
        name="moba_prompt",
    )(far, qt, kb, vt, kmean, bias)


def _moba_sample_kernel(pt_ref, q_ref, kn_ref, vn_ref, bias_ref, *refs, n_pages):
    del pt_ref
    kp = refs[:n_pages]
    vp = refs[n_pages:2 * n_pages]
    o_ref, e_scr, en_scr, l_scr = refs[2 * n_pages:]
    rows = H_A * SAMPLE_PAD
    ppb = MOBA_BLOCK // PAGE_SIZE
    n_blocks = n_pages // ppb
    past = n_pages * PAGE_SIZE
    pad_rows = LANES - SAMPLE_PAD
    step = pl.program_id(0)
    slot = lax.rem(step, 2)
    rowh = _div_pow2(lax.broadcasted_iota(jnp.int32, (rows, D_A), 0), SAMPLE_PAD)
    laneh = _div_pow2(lax.broadcasted_iota(jnp.int32, (rows, D_A), 1), DH_A)
    own = rowh == laneh

    @pl.when(step == 0)
    def _():
        e_scr[1] = jnp.zeros((rows, past), bf16)
        en_scr[1] = jnp.zeros((rows, LANES), bf16)
        l_scr[1] = jnp.ones((rows, LANES), f32)

    vn = jnp.concatenate([vn_ref[...], jnp.zeros((pad_rows, D_A), f32)], axis=0)
    vt = jnp.concatenate([vp[p][...].astype(bf16) for p in range(n_pages)], axis=1)
    acc = lax.dot_general(e_scr[1 - slot], vt, (((1,), (1,)), ((), ())), preferred_element_type=f32)
    acc = acc + jnp.dot(en_scr[1 - slot], vn.astype(bf16), preferred_element_type=f32)
    o = jnp.where(own, acc / l_scr[1 - slot][:, 0:1], 0.0)
    out = o[0:SAMPLE_PAD]
    for h in range(1, H_A):
        out = out + o[h * SAMPLE_PAD:(h + 1) * SAMPLE_PAD]
    o_ref[...] = out

    qe = jnp.where(own, jnp.concatenate([q_ref[...]] * H_A, axis=0), 0.0)
    qeb = (qe * DH_A ** -0.5).astype(bf16)

    lane_f = lax.broadcasted_iota(jnp.int32, (D_A, LANES), 1)
    kmean = jnp.zeros((D_A, LANES), f32)
    for j in range(n_blocks):
        ksum = kp[j * ppb][...]
        for p in range(1, ppb):
            ksum = ksum + kp[j * ppb + p][...]
        kmean = jnp.where(lane_f == j, jnp.sum(ksum, axis=1, keepdims=True) / MOBA_BLOCK, kmean)
    gate = jnp.dot(qe, kmean, precision=HIGHEST, preferred_element_type=f32)
    lane_r = lax.broadcasted_iota(jnp.int32, (rows, LANES), 1)
    lane_rf = lane_r.astype(f32)
    g = jnp.where(lane_r < n_blocks, gate, -jnp.inf)
    chosen = jnp.zeros((rows, LANES), jnp.bool_)
    for _ in range(MOBA_TOPK):
        m = jnp.max(g, axis=1, keepdims=True)
        first = jnp.min(jnp.where(g == m, lane_rf, float(LANES)), axis=1, keepdims=True)
        hit = lane_rf == first
        chosen = jnp.logical_or(chosen, hit)
        g = jnp.where(hit, -jnp.inf, g)
    sel = jnp.where(chosen, 0.0, NEG)

    kn = jnp.concatenate([kn_ref[...], jnp.zeros((pad_rows, D_A), f32)], axis=0)
    kt = jnp.concatenate([kp[p][...].astype(bf16) for p in range(n_pages)], axis=1)
    sel_past = jnp.concatenate([jnp.broadcast_to(sel[:, j:j + 1], (rows, MOBA_BLOCK)) for j in range(n_blocks)],
                               axis=1)
    s_past = bias_ref[:, :past] + sel_past + jnp.dot(qeb, kt, preferred_element_type=f32)
    s_new = _dot_nt(qeb, kn) + bias_ref[:, past:past + LANES]
    m = jnp.maximum(jnp.max(s_past, axis=-1, keepdims=True), jnp.max(s_new, axis=-1, keepdims=True))
    e_past = jnp.exp(s_past - m)
    e_new = jnp.exp(s_new - m)
    l = jnp.sum(e_past, axis=-1, keepdims=True) + jnp.sum(e_new, axis=-1, keepdims=True)
    e_scr[slot] = e_past.astype(bf16)
    en_scr[slot] = e_new.astype(bf16)
    l_scr[slot] = jnp.broadcast_to(l, (rows, LANES))


def _moba_sample(q, kn, vn, cache_k, cache_v, page_table, bias, layer):
    nseq, n_pages = page_table.shape
    assert (n_pages * PAGE_SIZE) % MOBA_BLOCK == 0 and (n_pages * PAGE_SIZE) // MOBA_BLOCK >= MOBA_TOPK
    rows = H_A * SAMPLE_PAD
    cur = lambda b: jnp.minimum(b, nseq - 1)
    prev = lambda b: jnp.maximum(b - 1, 0)
    seq = lambda which: pl.BlockSpec((None, SAMPLE_PAD, D_A), lambda b, pt: (which(b), 0, 0))
    page = lambda which, p: pl.BlockSpec((None, None, D_A, PAGE_SIZE),
                                         lambda b, pt: (layer, pt[which(b), p], 0, 0))
    grid_spec = pltpu.PrefetchScalarGridSpec(
        num_scalar_prefetch=1,
        grid=(nseq + 1,),
        in_specs=[seq(cur), seq(cur), seq(prev), pl.BlockSpec(bias.shape, lambda b, pt: (0, 0))]
        + [page(cur, p) for p in range(n_pages)] + [page(prev, p) for p in range(n_pages)],
        out_specs=seq(prev),
        scratch_shapes=[pltpu.VMEM((2, rows, n_pages * PAGE_SIZE), bf16), pltpu.VMEM((2, rows, LANES), bf16),
                        pltpu.VMEM((2, rows, LANES), f32)],
    )
    return pl.pallas_call(
        functools.partial(_moba_sample_kernel, n_pages=n_pages),
        out_shape=jax.ShapeDtypeStruct((nseq, SAMPLE_PAD, D_A), f32),
        grid_spec=grid_spec,
        compiler_params=_cparams("arbitrary"),
        name="moba_sample",
    )(page_table, q, kn, vn, bias, *([cache_k] * n_pages), *([cache_v] * n_pages))


def _gdn_kernel(x_ref, ba_ref, z_ref, conv0_ref, s0_ref, cw_ref, avec_ref, dtvec_ref, gon_ref,
                o_ref, s_ref, xs_scr, *, chunk, n_valid):
    t = pl.program_id(1)
    nb, tt = x_ref.shape[0], x_ref.shape[1]

    @pl.when(t == 0)
    def _():
        s_ref[...] = s0_ref[...]
        xs_scr[:, 0:SUBLANES, :] = conv0_ref[...]

    xs_scr[:, SUBLANES:SUBLANES + tt, :] = x_ref[...]

    ci = lax.broadcasted_iota(jnp.int32, (chunk, chunk), 0)
    cj = lax.broadcasted_iota(jnp.int32, (chunk, chunk), 1)
    causal = ci >= cj
    strict = ci > cj
    tri = causal.astype(f32)
    eye = (ci == cj).astype(f32)
    cw = cw_ref[...]
    neg_a = -jnp.exp(avec_ref[...])
    merges = []
    for b in range(int(math.log2(chunk))):
        same_pair = lax.shift_right_logical(ci, b + 1) == lax.shift_right_logical(cj, b + 1)
        row_hi = jnp.bitwise_and(lax.shift_right_logical(ci, b), 1) == 1
        col_lo = jnp.bitwise_and(lax.shift_right_logical(cj, b), 1) == 0
        merges.append(jnp.logical_and(same_pair, jnp.logical_and(row_hi, col_lo)))

    nc = tt // chunk
    qs, ks, vs, zs, bcols, gcols = [], [], [], [], [], []
    base = SUBLANES - (CONV_W - 1)
    for n in range(nb):
        u = xs_scr[n, base:base + tt, :] * cw[0:1]
        for j in range(1, CONV_W):
            u = u + xs_scr[n, base + j:base + j + tt, :] * cw[j:j + 1]
        u = _silu(u)
        ba = ba_ref[n]
        beta = jax.nn.sigmoid(ba)
        g = neg_a * _softplus(ba + dtvec_ref[...])
        if n_valid < tt:
            keep = lax.broadcasted_iota(jnp.int32, (tt, LANES), 0) < n_valid
            beta = jnp.where(keep, beta, 0.0)
            g = jnp.where(keep, g, 0.0)
        for c in range(nc):
            rows = slice(c * chunk, (c + 1) * chunk)
            gcum = jnp.dot(tri, g[rows], precision=HIGHEST, preferred_element_type=f32)
            for h in range(H_B):
                qs.append(u[rows, h * DK:(h + 1) * DK])
                ks.append(u[rows, H_B * DK + h * DK:H_B * DK + (h + 1) * DK])
                vs.append(u[rows, 2 * H_B * DK + h * DV:2 * H_B * DK + (h + 1) * DV])
                zs.append(z_ref[n, rows, h * DV:(h + 1) * DV])
                bcols.append(beta[rows, h:h + 1])
                gcols.append(gcum[:, H_B + h:H_B + h + 1])
    q, k, v, z = jnp.stack(qs), jnp.stack(ks), jnp.stack(vs), jnp.stack(zs)
    bcol, gcol = jnp.stack(bcols), jnp.stack(gcols)
    q = q * lax.rsqrt(jnp.sum(q * q, axis=-1, keepdims=True) + EPS) * DK ** -0.5
    k = k * lax.rsqrt(jnp.sum(k * k, axis=-1, keepdims=True) + EPS)
    grow = jnp.sum(eye[None] * gcol, axis=1, keepdims=True)
    decay = jnp.exp(jnp.where(causal[None], gcol - grow, NEG))
    kb = k * bcol
    lower = jnp.where(strict[None], _bmm_nt(kb, k) * decay, 0.0)
    inv = eye[None] - jnp.where(merges[0][None], lower, 0.0)
    for merge in merges[1:]:
        inv = inv - _bmm(_bmm(inv, jnp.where(merge[None], lower, 0.0)), inv)
    sol = _bmm(inv, jnp.concatenate([v * bcol, kb * jnp.exp(gcol)], axis=2))
    uu, ww = sol[:, :, :DV], sol[:, :, DV:]
    qk = jnp.where(causal[None], _bmm_nt(q, k) * decay, 0.0)
    qe = q * jnp.exp(gcol)
    g_last = gcol[:, chunk - 1:chunk, :]
    ke = k * jnp.exp(g_last - gcol)
    carry = jnp.exp(g_last)

    def at_chunk(a, c):
        return a.reshape((nb, nc, H_B) + a.shape[1:])[:, c].reshape((nb * H_B,) + a.shape[1:])

    state = s_ref[...].reshape(nb * H_B, DK, DV)
    for c in range(nc):
        v_new = at_chunk(uu, c) - _bmm(at_chunk(ww, c), state)
        o = _bmm(at_chunk(qe, c), state) + _bmm(at_chunk(qk, c), v_new)
        state = state * at_chunk(carry, c) + _bmm_tn(at_chunk(ke, c), v_new)
        o = o * lax.rsqrt(jnp.mean(o * o, axis=-1, keepdims=True) + EPS) * gon_ref[...]
        o = o * _silu(at_chunk(z, c))
        for n in range(nb):
            for h in range(H_B):
                o_ref[n, c * chunk:(c + 1) * chunk, h * DV:(h + 1) * DV] = o[n * H_B + h].astype(o_ref.dtype)
    s_ref[...] = state.reshape(nb, H_B, DK, DV)

    xs_scr[:, 0:SUBLANES, :] = xs_scr[:, tt:tt + SUBLANES, :]


def _gdn(xb, ba, zb, conv0, s0, layer, conv_w, a_log, dt_bias, g_onorm, *, nb, tt, chunk, n_valid):
    nseq, t_len = xb.shape[0], xb.shape[1]
    lane_vec = lambda v: jnp.zeros((1, LANES), f32).at[0, H_B:2 * H_B].set(v)
    tok = lambda w: pl.BlockSpec((nb, tt, w), lambda g, t: (g, t, 0))
    const = lambda shape: pl.BlockSpec(shape, lambda g, t: (0,) * len(shape))
    return pl.pallas_call(
        functools.partial(_gdn_kernel, chunk=chunk, n_valid=n_valid),
        out_shape=(jax.ShapeDtypeStruct((nseq, t_len, D_B), zb.dtype),
                   jax.ShapeDtypeStruct((nseq, H_B, DK, DV), f32)),
        grid=(nseq // nb, t_len // tt),
        in_specs=[tok(CONV_DIM), tok(LANES), tok(D_B),
                  pl.BlockSpec((None, nb, SUBLANES, CONV_DIM), lambda g, t: (layer, g, 0, 0)),
                  pl.BlockSpec((None, nb, H_B, DK, DV), lambda g, t: (layer, g, 0, 0, 0)),
                  const((CONV_W, CONV_DIM)), const((1, LANES)), const((1, LANES)), const((1, DV))],
        out_specs=(tok(D_B), pl.BlockSpec((nb, H_B, DK, DV), lambda g, t: (g, 0, 0, 0))),
        scratch_shapes=[pltpu.VMEM((nb, tt + SUBLANES, CONV_DIM), f32)],
        compiler_params=_cparams("arbitrary", "arbitrary"),
        name="gdn_%d" % tt,
    )(xb, ba, zb, conv0, s0, conv_w, lane_vec(a_log), lane_vec(dt_bias), g_onorm.reshape(1, DV))


def _outproj_kernel(oa_ref, za_ref, ob_ref, ga_ref, gb_ref, x_ref, gate_ref, wpa_ref, wpb_ref, wout_ref,
                    gpost_ref, y_ref, *, prompt):
    oa = oa_ref[...].T if prompt else oa_ref[...]
    oa = oa * _silu(za_ref[...])
    merged = (jax.nn.sigmoid(ga_ref[...]) * _dot(oa, wpa_ref[...])
              + jax.nn.sigmoid(gb_ref[...]) * _dot(ob_ref[...], wpb_ref[...]))
    y = _dot(merged, wout_ref[...])
    y = y * lax.rsqrt(jnp.mean(y * y, axis=-1, keepdims=True) + EPS) * gpost_ref[...]
    y_ref[...] = x_ref[...] + gate_ref[...] * y


def _outproj(oa, za, ob, ga, gb, x, gate, w_pa, w_pb, w_out, g_post, *, prompt, tiles_per_seq=None):
    n = x.shape[0]
    tm = TOKEN_TILE
    row = lambda i: (i, 0)
    tok = lambda w: pl.BlockSpec((tm, w), row)
    const = lambda shape: pl.BlockSpec(shape, lambda i: (0, 0))
    if prompt:
        oa_spec = pl.BlockSpec((None, D_A, tm), lambda i: (i, 0, 0))
        gate_spec = pl.BlockSpec((None, 1, D_MODEL), lambda i: (i // tiles_per_seq, 0, 0))
    else:
        oa_spec = tok(D_A)
        gate_spec = tok(D_MODEL)
    return pl.pallas_call(
        functools.partial(_outproj_kernel, prompt=prompt),
        out_shape=jax.ShapeDtypeStruct((n, D_MODEL), f32),
        grid=(n // tm,),
        in_specs=[oa_spec, tok(D_A), tok(D_B), tok(D_MODEL), tok(D_MODEL), tok(D_MODEL), gate_spec,
                  const((D_A, D_MODEL)), const((D_B, D_MODEL)), const((D_MODEL, D_MODEL)), const((1, D_MODEL))],
        out_specs=tok(D_MODEL),
        compiler_params=_cparams("arbitrary"),
        name="outproj_prompt" if prompt else "outproj_sample",
    )(oa, za, ob, ga, gb, x, gate, w_pa, w_pb, w_out, g_post.reshape(1, D_MODEL))


def _pack_w_in(w_in):
    depth = w_in.shape[0]
    n_ba = 2 * H_B
    ba = jnp.zeros((depth, D_MODEL, LANES), w_in.dtype).at[:, :, :n_ba].set(w_in[:, :, W_MAIN:W_MAIN + n_ba])
    return jnp.concatenate([w_in[:, :, :W_MAIN], ba, w_in[:, :, W_MAIN + n_ba:]], axis=-1).astype(bf16)


def _prompt_layer(x, mod, lw, bias, far, zero_conv, zero_state):
    bsz, t_len, _ = x.shape
    n = bsz * t_len
    nb = t_len // MOBA_BLOCK
    g_pre, g_post, w_cat, conv_w, a_log, dt_bias, g_onorm, w_pa, w_pb, w_out = lw
    shift, scale, gate = (mod[:, None, j * D_MODEL:(j + 1) * D_MODEL] for j in range(3))
    xf = x.reshape(n, D_MODEL)
    k, v, kb, qt, vt, km, za, xb, zb, ba, ga, gb = _inproj(xf, shift, scale, g_pre, w_cat, prompt=True,
                                                             tiles_per_seq=nb)
    ot = _moba_prompt(qt, kb.reshape(bsz, nb, MOBA_BLOCK, D_A), vt.reshape(bsz, nb, D_A, MOBA_BLOCK),
                      km.reshape(bsz, nb, D_A), bias, far)
    xb3 = xb.reshape(bsz, t_len, CONV_DIM)
    ob, s_new = _gdn(xb3, ba.reshape(bsz, t_len, LANES), zb.reshape(bsz, t_len, D_B), zero_conv, zero_state, 0,
                     conv_w, a_log, dt_bias, g_onorm, nb=math.gcd(bsz, 2), tt=TOKEN_TILE,
                     chunk=math.gcd(t_len, DELTA_CHUNK),
                     n_valid=TOKEN_TILE)
    y = _outproj(ot, za, ob.reshape(n, D_B), ga, gb, xf, gate, w_pa, w_pb, w_out, g_post, prompt=True,
                 tiles_per_seq=nb)
    heads = lambda a: jnp.transpose(a.reshape(bsz, H_A, DH_A, t_len), (0, 3, 1, 2))
    return y.reshape(bsz, t_len, D_MODEL), heads(k), heads(v), s_new, xb3[:, t_len - (CONV_W - 1):]


def _sample_layer(x, mod, lw, bias, cache_k, cache_v, page_table, state_delta, conv_pad, layer, dec_t):
    nseq = x.shape[0]
    n = nseq * SAMPLE_PAD
    g_pre, g_post, w_cat, conv_w, a_log, dt_bias, g_onorm, w_pa, w_pb, w_out = lw
    per_tok = jnp.repeat(mod, SAMPLE_PAD, axis=0)
    shift, scale, gate = (per_tok[:, j * D_MODEL:(j + 1) * D_MODEL] for j in range(3))
    xf = x.reshape(n, D_MODEL)
    q, k, v, za, xb, zb, ba, ga, gb = _inproj(xf, shift, scale, g_pre, w_cat, prompt=False)
    seq3 = lambda a: a.reshape(nseq, SAMPLE_PAD, a.shape[-1])
    oa = _moba_sample(seq3(q), seq3(k), seq3(v), cache_k, cache_v, page_table, bias, layer)
    xb3 = seq3(xb)
    ob, s_new = _gdn(xb3, seq3(ba), seq3(zb), conv_pad, state_delta, layer, conv_w, a_log, dt_bias, g_onorm,
                     nb=SUBLANES, tt=SAMPLE_PAD, chunk=SAMPLE_PAD, n_valid=dec_t)
    y = _outproj(oa.reshape(n, D_A), za, ob.reshape(n, D_B), ga, gb, xf, gate, w_pa, w_pb, w_out, g_post,
                 prompt=False)
    heads = lambda a: seq3(a)[:, :dec_t].reshape(nseq, dec_t, H_A, DH_A)
    return y.reshape(nseq, SAMPLE_PAD, D_MODEL), heads(k), heads(v), s_new, xb3[:, dec_t - (CONV_W - 1):dec_t]


def kernel(x_prompt, x_sample, c_prompt, c_sample, cache_k, cache_v, state_delta, state_conv, page_table, rel_bias, w_ada, b_ada, g_pre, g_post, w_in, conv_w, a_log, dt_bias, g_onorm, w_pa, w_pb, w_out):
    bsz, seq, _ = x_prompt.shape
    dec_b, dec_t, _ = x_sample.shape
    depth = w_in.shape[0]
    n_pages = page_table.shape[1]
    past_len = n_pages * PAGE_SIZE
    assert seq % MOBA_BLOCK == 0 and seq >= CONV_W - 1 and CONV_W - 1 <= dec_t <= SAMPLE_PAD
    assert dec_b % SUBLANES == 0

    bias_p = _bias_prompt(rel_bias)
    bias_s, far = _bias_sample(rel_bias, past_len, dec_t)
    far = far[:, 0]
    c_all = jnp.concatenate([c_prompt, c_sample], axis=0)
    c_all = jnp.pad(c_all, ((0, -c_all.shape[0] % (2 * SUBLANES)), (0, 0)))
    mod = _modulation(c_all, w_ada, b_ada)
    w_cat = _pack_w_in(w_in)
    w_pa_b, w_pb_b, w_out_b = w_pa.astype(bf16), w_pb.astype(bf16), w_out.astype(bf16)
    ck = jnp.transpose(cache_k, (0, 1, 3, 4, 2)).reshape(cache_k.shape[0], cache_k.shape[1], D_A, PAGE_SIZE)
    cv = jnp.transpose(cache_v, (0, 1, 3, 4, 2)).reshape(cache_v.shape[0], cache_v.shape[1], D_A, PAGE_SIZE)
    conv_pad = jnp.pad(state_conv, ((0, 0), (0, 0), (SUBLANES - (CONV_W - 1), 0), (0, 0)))
    zero_conv = jnp.zeros((1, bsz, SUBLANES, CONV_DIM), f32)
    zero_state = jnp.zeros((1, bsz, H_B, DK, DV), f32)

    hp = x_prompt
    hs = jnp.pad(x_sample, ((0, 0), (0, SAMPLE_PAD - dec_t), (0, 0)))
    outs = [[] for _ in range(8)]
    for l in range(depth):
        lw = (g_pre[l], g_post[l], w_cat[l], conv_w[l], a_log[l], dt_bias[l], g_onorm[l],
              w_pa_b[l], w_pb_b[l], w_out_b[l])
        hp, kp, vp, sp, cp = _prompt_layer(hp, mod[l, :bsz], lw, bias_p, far, zero_conv, zero_state)
        hs, ks, vs, ss, cs = _sample_layer(hs, mod[l, bsz:bsz + dec_b], lw, bias_s, ck, cv, page_table, state_delta,
                                           conv_pad, l, dec_t)
        for lst, val in zip(outs, (kp, vp, ks, vs, sp, ss, cp, cs)):
            lst.append(val)
    return (hp, hs[:, :dec_t]) + tuple(jnp.stack(o) for o in outs)
```

```python
import functools
import math

import jax
import jax.numpy as jnp
from jax import lax
from jax.experimental import pallas as pl
from jax.experimental.pallas import tpu as pltpu

f32 = jnp.float32
bf16 = jnp.bfloat16
HIGHEST = lax.Precision.HIGHEST

D_MODEL = 1024
D_A = D_MODEL // 2
DH_A = 64
H_A = D_A // DH_A
MOBA_BLOCK = 256
MOBA_TOPK = 3
N_BUCKETS = 32
MAX_DISTANCE = 1024
D_B = D_MODEL // 2
DK = 128
DV = 128
H_B = D_B // DV
CONV_W = 4
CONV_DIM = 2 * H_B * DK + H_B * DV
DELTA_CHUNK = 64
PAGE_SIZE = 128
EPS = 1e-6

LANES = 128
SUBLANES = 8
NEG = -1e30
LOG2E = math.log2(math.e)
TOKEN_TILE = MOBA_BLOCK
SAMPLE_PAD = SUBLANES
NEAR_TILES = MAX_DISTANCE // MOBA_BLOCK + 1
MOBA_HEAD_GROUP = 8
DEN_ROWS = 2 * SUBLANES
W_MAIN = 4 * D_A + CONV_DIM + H_B * DV
W_BA = W_MAIN
W_GATES = W_MAIN + LANES
W_TOTAL = W_GATES + 2 * D_MODEL
VMEM_LIMIT = 48 * 1024 * 1024


def _cparams(*sem):
    return pltpu.CompilerParams(dimension_semantics=sem, vmem_limit_bytes=VMEM_LIMIT)


def _dot(a, b):
    return jnp.dot(a.astype(bf16), b.astype(bf16), preferred_element_type=f32)


def _dot_nt(a, b):
    return lax.dot_general(a.astype(bf16), b.astype(bf16), (((1,), (1,)), ((), ())), preferred_element_type=f32)


def _dot_tn(a, b):
    return lax.dot_general(a.astype(bf16), b.astype(bf16), (((0,), (0,)), ((), ())), preferred_element_type=f32)


def _bmm(a, b):
    return lax.dot_general(a.astype(bf16), b.astype(bf16), (((2,), (1,)), ((0,), (0,))), preferred_element_type=f32)


def _bmm_nt(a, b):
    return lax.dot_general(a.astype(bf16), b.astype(bf16), (((2,), (2,)), ((0,), (0,))), preferred_element_type=f32)


def _bmm_tn(a, b):
    return lax.dot_general(a.astype(bf16), b.astype(bf16), (((1,), (1,)), ((0,), (0,))), preferred_element_type=f32)


def _silu(x):
    return x * jax.nn.sigmoid(x)


def _softplus(x):
    return jnp.maximum(x, 0.0) + jnp.log(1.0 + jnp.exp(-jnp.abs(x)))


def _div_pow2(x, n):
    return lax.shift_right_logical(x, int(math.log2(n)))


def _rel_bucket(dist):
    n = jnp.maximum(dist, 0)
    max_exact = N_BUCKETS // 2
    nf = jnp.maximum(n, 1).astype(f32)
    large = max_exact + (jnp.log(nf / max_exact) / math.log(MAX_DISTANCE / max_exact)
                         * (N_BUCKETS - max_exact)).astype(jnp.int32)
    large = jnp.minimum(large, N_BUCKETS - 1)
    return jnp.where(n < max_exact, n, large)


def _bias_prompt_kernel(rb_ref, o_ref):
    d = pl.program_id(0)
    h = pl.program_id(1)
    r = lax.broadcasted_iota(jnp.int32, (MOBA_BLOCK, MOBA_BLOCK), 0)
    c = lax.broadcasted_iota(jnp.int32, (MOBA_BLOCK, MOBA_BLOCK), 1)
    dist = d * MOBA_BLOCK + c - r
    bucket = _rel_bucket(dist)
    acc = jnp.zeros((MOBA_BLOCK, MOBA_BLOCK), f32)
    for b in range(N_BUCKETS):
        acc = jnp.where(bucket == b, rb_ref[b, h], acc)
    o_ref[...] = jnp.where(dist >= 0, acc * LOG2E, NEG)


def _bias_prompt(rel_bias):
    return pl.pallas_call(
        _bias_prompt_kernel,
        out_shape=jax.ShapeDtypeStruct((NEAR_TILES, H_A, MOBA_BLOCK, MOBA_BLOCK), f32),
        grid=(NEAR_TILES, H_A),
        in_specs=[pl.BlockSpec(memory_space=pltpu.SMEM)],
        out_specs=pl.BlockSpec((None, None, MOBA_BLOCK, MOBA_BLOCK), lambda d, h: (d, h, 0, 0)),
        compiler_params=_cparams("arbitrary", "arbitrary"),
        name="bias_prompt",
    )(rel_bias)


def _bias_sample_kernel(rb_ref, o_ref, far_ref, *, past_len, dec_t):
    rows = H_A * SAMPLE_PAD
    width = past_len + LANES
    row = lax.broadcasted_iota(jnp.int32, (rows, width), 0)
    lane = lax.broadcasted_iota(jnp.int32, (rows, width), 1)
    rowh = _div_pow2(row, SAMPLE_PAD)
    tq = jnp.minimum(row - rowh * SAMPLE_PAD, dec_t - 1)
    tn = lane - past_len
    is_new = lane >= past_len
    dist = jnp.where(is_new, tq - tn, past_len + tq - lane)
    valid = jnp.logical_or(jnp.logical_not(is_new), jnp.logical_and(tn <= tq, tn < dec_t))
    bucket = _rel_bucket(dist)
    far_bucket = _rel_bucket(jnp.full((H_A, LANES), MAX_DISTANCE, jnp.int32))
    far_row = lax.broadcasted_iota(jnp.int32, (H_A, LANES), 0)

    def body(b, carry):
        acc, far = carry
        for h in range(H_A):
            v = rb_ref[b, h]
            acc = jnp.where(jnp.logical_and(bucket == b, rowh == h), v, acc)
            far = jnp.where(jnp.logical_and(far_bucket == b, far_row == h), v, far)
        return acc, far

    acc, far = lax.fori_loop(0, N_BUCKETS, body, (jnp.zeros((rows, width), f32), jnp.zeros((H_A, LANES), f32)))
    o_ref[...] = jnp.where(valid, acc, NEG)
    far_ref[...] = far * LOG2E


def _bias_sample(rel_bias, past_len, dec_t):
    return pl.pallas_call(
        functools.partial(_bias_sample_kernel, past_len=past_len, dec_t=dec_t),
        out_shape=(jax.ShapeDtypeStruct((H_A * SAMPLE_PAD, past_len + LANES), f32),
                   jax.ShapeDtypeStruct((H_A, LANES), f32)),
        in_specs=[pl.BlockSpec(memory_space=pltpu.SMEM)],
        name="bias_sample",
    )(rel_bias)


def _mod_kernel(c_ref, w_ref, b_ref, o_ref):
    o_ref[...] = _dot(_silu(c_ref[...]), w_ref[...]) + b_ref[...]


def _modulation(c_all, w_ada, b_ada):
    depth = w_ada.shape[0]
    rows = c_all.shape[0]
    return pl.pallas_call(
        _mod_kernel,
        out_shape=jax.ShapeDtypeStruct((depth, rows, 3 * D_MODEL), f32),
        grid=(depth, 3),
        in_specs=[pl.BlockSpec((rows, D_MODEL), lambda l, j: (0, 0)),
                  pl.BlockSpec((None, D_MODEL, D_MODEL), lambda l, j: (l, 0, j)),
                  pl.BlockSpec((None, 1, D_MODEL), lambda l, j: (l, 0, j))],
        out_specs=pl.BlockSpec((None, rows, D_MODEL), lambda l, j: (l, 0, j)),
        compiler_params=_cparams("arbitrary", "arbitrary"),
        name="modulation",
    )(c_all, w_ada, b_ada.reshape(depth, 1, 3 * D_MODEL))


def _inproj_kernel(x_ref, shift_ref, scale_ref, gpre_ref, w_ref, *outs, prompt):
    x = x_ref[...]
    h = x * lax.rsqrt(jnp.mean(x * x, axis=-1, keepdims=True) + EPS) * gpre_ref[...]
    hb = (h * (1.0 + scale_ref[...]) + shift_ref[...]).astype(bf16)

    def proj(lo, hi):
        return jnp.dot(hb, w_ref[:, lo:hi], preferred_element_type=f32)

    q = proj(0, D_A)
    k = proj(D_A, 2 * D_A)
    v = proj(2 * D_A, 3 * D_A)
    if prompt:
        k_ref, v_ref, kb_ref, qt_ref, vt_ref, km_ref, za_ref, xb_ref, zb_ref, ba_ref, ga_ref, gb_ref = outs
        vt = v.T
        k_ref[...] = k.T
        v_ref[...] = vt
        kb_ref[...] = k.astype(bf16)
        qt_ref[...] = q.T
        vt_ref[...] = vt.astype(bf16)
        km_ref[...] = jnp.mean(k, axis=0, keepdims=True)
    else:
        q_ref, k_ref, v_ref, za_ref, xb_ref, zb_ref, ba_ref, ga_ref, gb_ref = outs
        q_ref[...] = q
        k_ref[...] = k
        v_ref[...] = v
    za_ref[...] = proj(3 * D_A, 4 * D_A).astype(za_ref.dtype)
    xb_ref[...] = proj(4 * D_A, 4 * D_A + CONV_DIM)
    zb_ref[...] = proj(4 * D_A + CONV_DIM, W_MAIN).astype(zb_ref.dtype)
    ba_ref[...] = proj(W_BA, W_BA + LANES)
    ga_ref[...] = proj(W_GATES, W_GATES + D_MODEL).astype(ga_ref.dtype)
    gb_ref[...] = proj(W_GATES + D_MODEL, W_TOTAL).astype(gb_ref.dtype)


def _inproj(x, shift, scale, g_pre, w_cat, *, prompt, tiles_per_seq=None):
    n = x.shape[0]
    tm = TOKEN_TILE
    nt = n // tm
    row = lambda i: (i, 0)
    if prompt:
        mod_spec = pl.BlockSpec((None, 1, D_MODEL), lambda i: (i // tiles_per_seq, 0, 0))
    else:
        mod_spec = pl.BlockSpec((tm, D_MODEL), row)
    tok = lambda w, dt=f32: (jax.ShapeDtypeStruct((n, w), dt), pl.BlockSpec((tm, w), row))
    gdt = bf16 if prompt else f32
    common = [tok(D_A, gdt), tok(CONV_DIM), tok(D_B, gdt), tok(LANES), tok(D_MODEL, gdt), tok(D_MODEL, gdt)]
    if prompt:
        seq_t = (jax.ShapeDtypeStruct((nt // tiles_per_seq, D_A, tiles_per_seq * tm), f32),
                 pl.BlockSpec((None, D_A, tm), lambda i: (i // tiles_per_seq, 0, i % tiles_per_seq)))
        outs = [seq_t, seq_t, tok(D_A, bf16),
                (jax.ShapeDtypeStruct((nt, D_A, tm), f32), pl.BlockSpec((None, D_A, tm), lambda i: (i, 0, 0))),
                (jax.ShapeDtypeStruct((nt, D_A, tm), bf16), pl.BlockSpec((None, D_A, tm), lambda i: (i, 0, 0))),
                (jax.ShapeDtypeStruct((nt, 1, D_A), f32), pl.BlockSpec((None, 1, D_A), lambda i: (i, 0, 0)))] + common
    else:
        outs = [tok(D_A), tok(D_A), tok(D_A)] + common
    return pl.pallas_call(
        functools.partial(_inproj_kernel, prompt=prompt),
        out_shape=tuple(o[0] for o in outs),
        grid=(nt,),
        in_specs=[pl.BlockSpec((tm, D_MODEL), row), mod_spec, mod_spec,
                  pl.BlockSpec((1, D_MODEL), lambda i: (0, 0)),
                  pl.BlockSpec((D_MODEL, W_TOTAL), lambda i: (0, 0), pipeline_mode=pl.Buffered(1))],
        out_specs=tuple(o[1] for o in outs),
        compiler_params=_cparams("arbitrary"),
        name="inproj_prompt" if prompt else "inproj_sample",
    )(x, shift, scale, g_pre.reshape(1, D_MODEL), w_cat)


def _moba_prompt_kernel(far_ref, qt_ref, kb_ref, vt_ref, km_ref, bias_ref, o_ref, sel_scr, far_scr, qm_scr, *, hg):
    grp = pl.program_id(1)
    i = pl.program_id(2)
    nb = km_ref.shape[0]
    pair_w = 2 * DH_A
    blk = lax.broadcasted_iota(jnp.int32, (nb, MOBA_BLOCK), 0)
    blkf = blk.astype(f32)
    feat = lax.broadcasted_iota(jnp.int32, (pair_w, MOBA_BLOCK), 0)
    for hh in range(hg):
        pair, lo = hh // 2, (hh % 2) * DH_A
        qt = qt_ref[pair * pair_w:(pair + 1) * pair_w, :]
        qm = jnp.where(jnp.logical_and(feat >= lo, feat < lo + DH_A), qt, 0.0)
        gate = jnp.dot(km_ref[:, pair * pair_w:(pair + 1) * pair_w], qm, precision=HIGHEST,
                       preferred_element_type=f32)
        g = jnp.where(blk < i, gate, -jnp.inf)
        chosen = blk == i
        for _ in range(MOBA_TOPK):
            m = jnp.max(g, axis=0, keepdims=True)
            first = jnp.min(jnp.where(g == m, blkf, float(nb)), axis=0, keepdims=True)
            hit = jnp.logical_and(blkf == first, m > -jnp.inf)
            chosen = jnp.logical_or(chosen, hit)
            g = jnp.where(hit, -jnp.inf, g)
        sel_scr[hh] = chosen.astype(f32)
        far_scr[hh] = jnp.full((1, MOBA_BLOCK), far_ref[grp * hg + hh], f32)
        qm_scr[hh] = (qm * (DH_A ** -0.5 * LOG2E)).astype(bf16)

    def step(j, carry, near):
        m, acc = carry
        kj = kb_ref[j]
        kj = jnp.stack([kj[:, (hh // 2) * pair_w:(hh // 2 + 1) * pair_w] for hh in range(hg)])
        vj = jnp.concatenate([vt_ref[j].reshape(hg, DH_A, MOBA_BLOCK),
                              jnp.ones((hg, DEN_ROWS, MOBA_BLOCK), bf16)], axis=1)
        s = _bmm(kj, qm_scr[...])
        if near:
            s = bias_ref[i - j] + s
            shift = 0.0
        else:
            shift = far_scr[...]
        ok = sel_scr[:, pl.ds(j, 1), :] > 0.5
        m_new = jnp.where(ok, jnp.maximum(m, jnp.max(s, axis=1, keepdims=True) + shift), m)
        alpha = jnp.exp2(m - m_new)
        p = jnp.exp2((s - jnp.where(ok, m_new - shift, -NEG)).astype(bf16))
        acc = alpha * acc + _bmm(vj, p)
        return m_new, acc

    init = (jnp.full((hg, 1, MOBA_BLOCK), NEG, f32), jnp.zeros((hg, DH_A + DEN_ROWS, MOBA_BLOCK), f32))
    def sweep(lo, hi, carry, near):
        pairs = lax.shift_right_logical(hi - lo, 1)
        carry = lax.fori_loop(0, pairs, lambda t, c: step(lo + 2 * t + 1, step(lo + 2 * t, c, near), near), carry)
        return lax.fori_loop(lo + 2 * pairs, hi, lambda j, c: step(j, c, near), carry)

    n_far = jnp.maximum(i - (NEAR_TILES - 1), 0)
    _, acc = sweep(n_far, i + 1, sweep(0, n_far, init, False), True)
    o_ref[...] = (acc[:, :DH_A, :] / acc[:, DH_A:DH_A + 1, :]).reshape(hg * DH_A, MOBA_BLOCK)


def _moba_prompt(qt, kb, vt, kmean, bias, far):
    bsz, nb = kb.shape[0], kb.shape[1]
    hg = MOBA_HEAD_GROUP
    gw = hg * DH_A
    return pl.pallas_call(
        functools.partial(_moba_prompt_kernel, hg=hg),
        out_shape=jax.ShapeDtypeStruct((bsz * nb, D_A, MOBA_BLOCK), f32),
        grid=(bsz, H_A // hg, nb),
        in_specs=[pl.BlockSpec(memory_space=pltpu.SMEM),
                  pl.BlockSpec((None, gw, MOBA_BLOCK), lambda b, g, i: (b * nb + i, g, 0)),
                  pl.BlockSpec((None, nb, MOBA_BLOCK, gw), lambda b, g, i: (b, 0, 0, g)),
                  pl.BlockSpec((None, nb, gw, MOBA_BLOCK), lambda b, g, i: (b, 0, g, 0)),
                  pl.BlockSpec((None, nb, gw), lambda b, g, i: (b, 0, g)),
                  pl.BlockSpec((NEAR_TILES, hg, MOBA_BLOCK, MOBA_BLOCK), lambda b, g, i: (0, g, 0, 0))],
        out_specs=pl.BlockSpec((None, gw, MOBA_BLOCK), lambda b, g, i: (b * nb + i, g, 0)),
        scratch_shapes=[pltpu.VMEM((hg, nb, MOBA_BLOCK), f32), pltpu.VMEM((hg, 1, MOBA_BLOCK), f32),
                        pltpu.VMEM((hg, 2 * DH_A, MOBA_BLOCK), bf16)],
        compiler_params=_cparams("arbitrary", "arbitrary", "arbitrary"),
        name="moba_prompt",
    )(far, qt, kb, vt, kmean, bias)


def _moba_sample_kernel(pt_ref, q_ref, kn_ref, vn_ref, bias_ref, *refs, n_pages):
    del pt_ref
    kp = refs[:n_pages]
    vp = refs[n_pages:2 * n_pages]
    o_ref, e_scr, en_scr, l_scr = refs[2 * n_pages:]
    rows = H_A * SAMPLE_PAD
    ppb = MOBA_BLOCK // PAGE_SIZE
    n_blocks = n_pages // ppb
    past = n_pages * PAGE_SIZE
    pad_rows = LANES - SAMPLE_PAD
    step = pl.program_id(0)
    slot = lax.rem(step, 2)
    rowh = _div_pow2(lax.broadcasted_iota(jnp.int32, (rows, D_A), 0), SAMPLE_PAD)
    laneh = _div_pow2(lax.broadcasted_iota(jnp.int32, (rows, D_A), 1), DH_A)
    own = rowh == laneh

    @pl.when(step == 0)
    def _():
        e_scr[1] = jnp.zeros((rows, past), bf16)
        en_scr[1] = jnp.zeros((rows, LANES), bf16)
        l_scr[1] = jnp.ones((rows, LANES), f32)

    vn = jnp.concatenate([vn_ref[...], jnp.zeros((pad_rows, D_A), f32)], axis=0)
    vt = jnp.concatenate([vp[p][...].astype(bf16) for p in range(n_pages)], axis=1)
    acc = lax.dot_general(e_scr[1 - slot], vt, (((1,), (1,)), ((), ())), preferred_element_type=f32)
    acc = acc + jnp.dot(en_scr[1 - slot], vn.astype(bf16), preferred_element_type=f32)
    o = jnp.where(own, acc / l_scr[1 - slot][:, 0:1], 0.0)
    out = o[0:SAMPLE_PAD]
    for h in range(1, H_A):
        out = out + o[h * SAMPLE_PAD:(h + 1) * SAMPLE_PAD]
    o_ref[...] = out

    qe = jnp.where(own, jnp.concatenate([q_ref[...]] * H_A, axis=0), 0.0)
    qeb = (qe * DH_A ** -0.5).astype(bf16)

    lane_f = lax.broadcasted_iota(jnp.int32, (D_A, LANES), 1)
    kmean = jnp.zeros((D_A, LANES), f32)
    for j in range(n_blocks):
        ksum = kp[j * ppb][...]
        for p in range(1, ppb):
            ksum = ksum + kp[j * ppb + p][...]
        kmean = jnp.where(lane_f == j, jnp.sum(ksum, axis=1, keepdims=True) / MOBA_BLOCK, kmean)
    gate = jnp.dot(qe, kmean, precision=HIGHEST, preferred_element_type=f32)
    lane_r = lax.broadcasted_iota(jnp.int32, (rows, LANES), 1)
    lane_rf = lane_r.astype(f32)
    g = jnp.where(lane_r < n_blocks, gate, -jnp.inf)
    chosen = jnp.zeros((rows, LANES), jnp.bool_)
    for _ in range(MOBA_TOPK):
        m = jnp.max(g, axis=1, keepdims=True)
        first = jnp.min(jnp.where(g == m, lane_rf, float(LANES)), axis=1, keepdims=True)
        hit = lane_rf == first
        chosen = jnp.logical_or(chosen, hit)
        g = jnp.where(hit, -jnp.inf, g)
    sel = jnp.where(chosen, 0.0, NEG)

    kn = jnp.concatenate([kn_ref[...], jnp.zeros((pad_rows, D_A), f32)], axis=0)
    kt = jnp.concatenate([kp[p][...].astype(bf16) for p in range(n_pages)], axis=1)
    sel_past = jnp.concatenate([jnp.broadcast_to(sel[:, j:j + 1], (rows, MOBA_BLOCK)) for j in range(n_blocks)],
                               axis=1)
    s_past = bias_ref[:, :past] + sel_past + jnp.dot(qeb, kt, preferred_element_type=f32)
    s_new = _dot_nt(qeb, kn) + bias_ref[:, past:past + LANES]
    m = jnp.maximum(jnp.max(s_past, axis=-1, keepdims=True), jnp.max(s_new, axis=-1, keepdims=True))
    e_past = jnp.exp(s_past - m)
    e_new = jnp.exp(s_new - m)
    l = jnp.sum(e_past, axis=-1, keepdims=True) + jnp.sum(e_new, axis=-1, keepdims=True)
    e_scr[slot] = e_past.astype(bf16)
    en_scr[slot] = e_new.astype(bf16)
    l_scr[slot] = jnp.broadcast_to(l, (rows, LANES))


def _moba_sample(q, kn, vn, cache_k, cache_v, page_table, bias, layer):
    nseq, n_pages = page_table.shape
    assert (n_pages * PAGE_SIZE) % MOBA_BLOCK == 0 and (n_pages * PAGE_SIZE) // MOBA_BLOCK >= MOBA_TOPK
    rows = H_A * SAMPLE_PAD
    cur = lambda b: jnp.minimum(b, nseq - 1)
    prev = lambda b: jnp.maximum(b - 1, 0)
    seq = lambda which: pl.BlockSpec((None, SAMPLE_PAD, D_A), lambda b, pt: (which(b), 0, 0))
    page = lambda which, p: pl.BlockSpec((None, None, D_A, PAGE_SIZE),
                                         lambda b, pt: (layer, pt[which(b), p], 0, 0))
    grid_spec = pltpu.PrefetchScalarGridSpec(
        num_scalar_prefetch=1,
        grid=(nseq + 1,),
        in_specs=[seq(cur), seq(cur), seq(prev), pl.BlockSpec(bias.shape, lambda b, pt: (0, 0))]
        + [page(cur, p) for p in range(n_pages)] + [page(prev, p) for p in range(n_pages)],
        out_specs=seq(prev),
        scratch_shapes=[pltpu.VMEM((2, rows, n_pages * PAGE_SIZE), bf16), pltpu.VMEM((2, rows, LANES), bf16),
                        pltpu.VMEM((2, rows, LANES), f32)],
    )
    return pl.pallas_call(
        functools.partial(_moba_sample_kernel, n_pages=n_pages),
        out_shape=jax.ShapeDtypeStruct((nseq, SAMPLE_PAD, D_A), f32),
        grid_spec=grid_spec,
        compiler_params=_cparams("arbitrary"),
        name="moba_sample",
    )(page_table, q, kn, vn, bias, *([cache_k] * n_pages), *([cache_v] * n_pages))


def _gdn_kernel(x_ref, ba_ref, z_ref, conv0_ref, s0_ref, cw_ref, avec_ref, dtvec_ref, gon_ref,
                o_ref, s_ref, xs_scr, *, chunk, n_valid):
    t = pl.program_id(1)
    nb, tt = x_ref.shape[0], x_ref.shape[1]

    @pl.when(t == 0)
    def _():
        s_ref[...] = s0_ref[...]
        xs_scr[:, 0:SUBLANES, :] = conv0_ref[...]

    xs_scr[:, SUBLANES:SUBLANES + tt, :] = x_ref[...]

    ci = lax.broadcasted_iota(jnp.int32, (chunk, chunk), 0)
    cj = lax.broadcasted_iota(jnp.int32, (chunk, chunk), 1)
    causal = ci >= cj
    strict = ci > cj
    tri = causal.astype(f32)
    eye = (ci == cj).astype(f32)
    cw = cw_ref[...]
    neg_a = -jnp.exp(avec_ref[...])
    merges = []
    for b in range(int(math.log2(chunk))):
        same_pair = lax.shift_right_logical(ci, b + 1) == lax.shift_right_logical(cj, b + 1)
        row_hi = jnp.bitwise_and(lax.shift_right_logical(ci, b), 1) == 1
        col_lo = jnp.bitwise_and(lax.shift_right_logical(cj, b), 1) == 0
        merges.append(jnp.logical_and(same_pair, jnp.logical_and(row_hi, col_lo)))

    nc = tt // chunk
    qs, ks, vs, zs, bcols, gcols = [], [], [], [], [], []
    for n in range(nb):
        slab = xs_scr[n]
        u = pltpu.roll(slab, CONV_W - 1, 0)[SUBLANES:, :] * cw[0:1]
        for j in range(1, CONV_W - 1):
            u = u + pltpu.roll(slab, CONV_W - 1 - j, 0)[SUBLANES:, :] * cw[j:j + 1]
        u = _silu(u + slab[SUBLANES:, :] * cw[CONV_W - 1:CONV_W])
        ba = ba_ref[n]
        beta = jax.nn.sigmoid(ba)
        g = neg_a * _softplus(ba + dtvec_ref[...])
        if n_valid < tt:
            keep = lax.broadcasted_iota(jnp.int32, (tt, LANES), 0) < n_valid
            beta = jnp.where(keep, beta, 0.0)
            g = jnp.where(keep, g, 0.0)
        for c in range(nc):
            rows = slice(c * chunk, (c + 1) * chunk)
            gcum = jnp.dot(tri, g[rows], precision=HIGHEST, preferred_element_type=f32)
            for h in range(H_B):
                qs.append(u[rows, h * DK:(h + 1) * DK])
                ks.append(u[rows, H_B * DK + h * DK:H_B * DK + (h + 1) * DK])
                vs.append(u[rows, 2 * H_B * DK + h * DV:2 * H_B * DK + (h + 1) * DV])
                zs.append(z_ref[n, rows, h * DV:(h + 1) * DV])
                bcols.append(beta[rows, h:h + 1])
                gcols.append(gcum[:, H_B + h:H_B + h + 1])
    q, k, v, z = jnp.stack(qs), jnp.stack(ks), jnp.stack(vs), jnp.stack(zs)
    bcol, gcol = jnp.stack(bcols), jnp.stack(gcols)
    q = q * (lax.rsqrt(jnp.sum(q * q, axis=-1, keepdims=True) + EPS) * DK ** -0.5)
    k = k * lax.rsqrt(jnp.sum(k * k, axis=-1, keepdims=True) + EPS)
    grow = jnp.sum(eye[None] * gcol, axis=1, keepdims=True)
    decay = jnp.exp(jnp.where(causal[None], gcol - grow, NEG))
    kb = k * bcol
    lower = jnp.where(strict[None], _bmm_nt(kb, k) * decay, 0.0)
    inv = eye[None] - jnp.where(merges[0][None], lower, 0.0)
    for merge in merges[1:]:
        inv = inv - _bmm(_bmm(inv, jnp.where(merge[None], lower, 0.0)), inv)
    sol = _bmm(inv, jnp.concatenate([v * bcol, kb * jnp.exp(gcol)], axis=2))
    uu, ww = sol[:, :, :DV], sol[:, :, DV:]
    qk = jnp.where(causal[None], _bmm_nt(q, k) * decay, 0.0)
    qe = q * jnp.exp(gcol)
    g_last = gcol[:, chunk - 1:chunk, :]
    ke = k * jnp.exp(g_last - gcol)
    carry = jnp.exp(g_last)

    def at_chunk(a, c):
        return a.reshape((nb, nc, H_B) + a.shape[1:])[:, c].reshape((nb * H_B,) + a.shape[1:])

    state = s_ref[...].reshape(nb * H_B, DK, DV)
    for c in range(nc):
        v_new = at_chunk(uu, c) - _bmm(at_chunk(ww, c), state)
        o = _bmm(at_chunk(qe, c), state) + _bmm(at_chunk(qk, c), v_new)
        state = state * at_chunk(carry, c) + _bmm_tn(at_chunk(ke, c), v_new)
        o = o * lax.rsqrt(jnp.mean(o * o, axis=-1, keepdims=True) + EPS) * gon_ref[...]
        o = o * _silu(at_chunk(z, c))
        for n in range(nb):
            for h in range(H_B):
                o_ref[n, c * chunk:(c + 1) * chunk, h * DV:(h + 1) * DV] = o[n * H_B + h].astype(o_ref.dtype)
    s_ref[...] = state.reshape(nb, H_B, DK, DV)

    xs_scr[:, 0:SUBLANES, :] = xs_scr[:, tt:tt + SUBLANES, :]


def _gdn(xb, ba, zb, conv0, s0, layer, conv_w, a_log, dt_bias, g_onorm, *, nb, tt, chunk, n_valid):
    nseq, t_len = xb.shape[0], xb.shape[1]
    lane_vec = lambda v: jnp.zeros((1, LANES), f32).at[0, H_B:2 * H_B].set(v)
    tok = lambda w: pl.BlockSpec((nb, tt, w), lambda g, t: (g, t, 0))
    const = lambda shape: pl.BlockSpec(shape, lambda g, t: (0,) * len(shape))
    return pl.pallas_call(
        functools.partial(_gdn_kernel, chunk=chunk, n_valid=n_valid),
        out_shape=(jax.ShapeDtypeStruct((nseq, t_len, D_B), zb.dtype),
                   jax.ShapeDtypeStruct((nseq, H_B, DK, DV), f32)),
        grid=(nseq // nb, t_len // tt),
        in_specs=[tok(CONV_DIM), tok(LANES), tok(D_B),
                  pl.BlockSpec((None, nb, SUBLANES, CONV_DIM), lambda g, t: (layer, g, 0, 0)),
                  pl.BlockSpec((None, nb, H_B, DK, DV), lambda g, t: (layer, g, 0, 0, 0)),
                  const((CONV_W, CONV_DIM)), const((1, LANES)), const((1, LANES)), const((1, DV))],
        out_specs=(tok(D_B), pl.BlockSpec((nb, H_B, DK, DV), lambda g, t: (g, 0, 0, 0))),
        scratch_shapes=[pltpu.VMEM((nb, tt + SUBLANES, CONV_DIM), f32)],
        compiler_params=_cparams("arbitrary", "arbitrary"),
        name="gdn_%d" % tt,
    )(xb, ba, zb, conv0, s0, conv_w, lane_vec(a_log), lane_vec(dt_bias), g_onorm.reshape(1, DV))


def _outproj_kernel(oa_ref, za_ref, ob_ref, ga_ref, gb_ref, x_ref, gate_ref, wpa_ref, wpb_ref, wout_ref,
                    gpost_ref, y_ref, *, prompt):
    oa = oa_ref[...].T if prompt else oa_ref[...]
    oa = oa * _silu(za_ref[...])
    merged = (jax.nn.sigmoid(ga_ref[...]) * _dot(oa, wpa_ref[...])
              + jax.nn.sigmoid(gb_ref[...]) * _dot(ob_ref[...], wpb_ref[...]))
    y = _dot(merged, wout_ref[...])
    y = y * lax.rsqrt(jnp.mean(y * y, axis=-1, keepdims=True) + EPS) * gpost_ref[...]
    y_ref[...] = x_ref[...] + gate_ref[...] * y


def _outproj(oa, za, ob, ga, gb, x, gate, w_pa, w_pb, w_out, g_post, *, prompt, tiles_per_seq=None):
    n = x.shape[0]
    tm = TOKEN_TILE
    row = lambda i: (i, 0)
    tok = lambda w: pl.BlockSpec((tm, w), row)
    const = lambda shape: pl.BlockSpec(shape, lambda i: (0, 0))
    if prompt:
        oa_spec = pl.BlockSpec((None, D_A, tm), lambda i: (i, 0, 0))
        gate_spec = pl.BlockSpec((None, 1, D_MODEL), lambda i: (i // tiles_per_seq, 0, 0))
    else:
        oa_spec = tok(D_A)
        gate_spec = tok(D_MODEL)
    return pl.pallas_call(
        functools.partial(_outproj_kernel, prompt=prompt),
        out_shape=jax.ShapeDtypeStruct((n, D_MODEL), f32),
        grid=(n // tm,),
        in_specs=[oa_spec, tok(D_A), tok(D_B), tok(D_MODEL), tok(D_MODEL), tok(D_MODEL), gate_spec,
                  const((D_A, D_MODEL)), const((D_B, D_MODEL)), const((D_MODEL, D_MODEL)), const((1, D_MODEL))],
        out_specs=tok(D_MODEL),
        compiler_params=_cparams("arbitrary"),
        name="outproj_prompt" if prompt else "outproj_sample",
    )(oa, za, ob, ga, gb, x, gate, w_pa, w_pb, w_out, g_post.reshape(1, D_MODEL))


def _pack_w_in(w_in):
    depth = w_in.shape[0]
    n_ba = 2 * H_B
    ba = jnp.zeros((depth, D_MODEL, LANES), w_in.dtype).at[:, :, :n_ba].set(w_in[:, :, W_MAIN:W_MAIN + n_ba])
    return jnp.concatenate([w_in[:, :, :W_MAIN], ba, w_in[:, :, W_MAIN + n_ba:]], axis=-1).astype(bf16)


def _prompt_layer(x, mod, lw, bias, far, zero_conv, zero_state):
    bsz, t_len, _ = x.shape
    n = bsz * t_len
    nb = t_len // MOBA_BLOCK
    g_pre, g_post, w_cat, conv_w, a_log, dt_bias, g_onorm, w_pa, w_pb, w_out = lw
    shift, scale, gate = (mod[:, None, j * D_MODEL:(j + 1) * D_MODEL] for j in range(3))
    xf = x.reshape(n, D_MODEL)
    k, v, kb, qt, vt, km, za, xb, zb, ba, ga, gb = _inproj(xf, shift, scale, g_pre, w_cat, prompt=True,
                                                             tiles_per_seq=nb)
    ot = _moba_prompt(qt, kb.reshape(bsz, nb, MOBA_BLOCK, D_A), vt.reshape(bsz, nb, D_A, MOBA_BLOCK),
                      km.reshape(bsz, nb, D_A), bias, far)
    xb3 = xb.reshape(bsz, t_len, CONV_DIM)
    ob, s_new = _gdn(xb3, ba.reshape(bsz, t_len, LANES), zb.reshape(bsz, t_len, D_B), zero_conv, zero_state, 0,
                     conv_w, a_log, dt_bias, g_onorm, nb=math.gcd(bsz, 2), tt=TOKEN_TILE,
                     chunk=math.gcd(t_len, DELTA_CHUNK),
                     n_valid=TOKEN_TILE)
    y = _outproj(ot, za, ob.reshape(n, D_B), ga, gb, xf, gate, w_pa, w_pb, w_out, g_post, prompt=True,
                 tiles_per_seq=nb)
    heads = lambda a: jnp.transpose(a.reshape(bsz, H_A, DH_A, t_len), (0, 3, 1, 2))
    return y.reshape(bsz, t_len, D_MODEL), heads(k), heads(v), s_new, xb3[:, t_len - (CONV_W - 1):]


def _sample_layer(x, mod, lw, bias, cache_k, cache_v, page_table, state_delta, conv_pad, layer, dec_t):
    nseq = x.shape[0]
    n = nseq * SAMPLE_PAD
    g_pre, g_post, w_cat, conv_w, a_log, dt_bias, g_onorm, w_pa, w_pb, w_out = lw
    per_tok = jnp.repeat(mod, SAMPLE_PAD, axis=0)
    shift, scale, gate = (per_tok[:, j * D_MODEL:(j + 1) * D_MODEL] for j in range(3))
    xf = x.reshape(n, D_MODEL)
    q, k, v, za, xb, zb, ba, ga, gb = _inproj(xf, shift, scale, g_pre, w_cat, prompt=False)
    seq3 = lambda a: a.reshape(nseq, SAMPLE_PAD, a.shape[-1])
    oa = _moba_sample(seq3(q), seq3(k), seq3(v), cache_k, cache_v, page_table, bias, layer)
    xb3 = seq3(xb)
    ob, s_new = _gdn(xb3, seq3(ba), seq3(zb), conv_pad, state_delta, layer, conv_w, a_log, dt_bias, g_onorm,
                     nb=SUBLANES, tt=SAMPLE_PAD, chunk=SAMPLE_PAD, n_valid=dec_t)
    y = _outproj(oa.reshape(n, D_A), za, ob.reshape(n, D_B), ga, gb, xf, gate, w_pa, w_pb, w_out, g_post,
                 prompt=False)
    heads = lambda a: seq3(a)[:, :dec_t].reshape(nseq, dec_t, H_A, DH_A)
    return y.reshape(nseq, SAMPLE_PAD, D_MODEL), heads(k), heads(v), s_new, xb3[:, dec_t - (CONV_W - 1):dec_t]


def kernel(x_prompt, x_sample, c_prompt, c_sample, cache_k, cache_v, state_delta, state_conv, page_table, rel_bias, w_ada, b_ada, g_pre, g_post, w_in, conv_w, a_log, dt_bias, g_onorm, w_pa, w_pb, w_out):
    bsz, seq, _ = x_prompt.shape
    dec_b, dec_t, _ = x_sample.shape
    depth = w_in.shape[0]
    n_pages = page_table.shape[1]
    past_len = n_pages * PAGE_SIZE
    assert seq % MOBA_BLOCK == 0 and seq >= CONV_W - 1 and CONV_W - 1 <= dec_t <= SAMPLE_PAD
    assert dec_b % SUBLANES == 0

    bias_p = _bias_prompt(rel_bias)
    bias_s, far = _bias_sample(rel_bias, past_len, dec_t)
    far = far[:, 0]
    c_all = jnp.concatenate([c_prompt, c_sample], axis=0)
    c_all = jnp.pad(c_all, ((0, -c_all.shape[0] % (2 * SUBLANES)), (0, 0)))
    mod = _modulation(c_all, w_ada, b_ada)
    w_cat = _pack_w_in(w_in)
    w_pa_b, w_pb_b, w_out_b = w_pa.astype(bf16), w_pb.astype(bf16), w_out.astype(bf16)
    ck = jnp.transpose(cache_k, (0, 1, 3, 4, 2)).reshape(cache_k.shape[0], cache_k.shape[1], D_A, PAGE_SIZE)
    cv = jnp.transpose(cache_v, (0, 1, 3, 4, 2)).reshape(cache_v.shape[0], cache_v.shape[1], D_A, PAGE_SIZE)
    conv_pad = jnp.pad(state_conv, ((0, 0), (0, 0), (SUBLANES - (CONV_W - 1), 0), (0, 0)))
    zero_conv = jnp.zeros((1, bsz, SUBLANES, CONV_DIM), f32)
    zero_state = jnp.zeros((1, bsz, H_B, DK, DV), f32)

    hp = x_prompt
    hs = jnp.pad(x_sample, ((0, 0), (0, SAMPLE_PAD - dec_t), (0, 0)))
    outs = [[] for _ in range(8)]
    for l in range(depth):
        lw = (g_pre[l], g_post[l], w_cat[l], conv_w[l], a_log[l], dt_bias[l], g_onorm[l],
              w_pa_b[l], w_pb_b[l], w_out_b[l])
        hp, kp, vp, sp, cp = _prompt_layer(hp, mod[l, :bsz], lw, bias_p, far, zero_conv, zero_state)
        hs, ks, vs, ss, cs = _sample_layer(hs, mod[l, bsz:bsz + dec_b], lw, bias_s, ck, cv, page_table, state_delta,
                                           conv_pad, l, dec_t)
        for lst, val in zip(outs, (kp, vp, ks, vs, sp, ss, cp, cs)):
            lst.append(val)
    return (hp, hs[:, :dec_t]) + tuple(jnp.stack(o) for o in outs)
```

```python
import functools
import math

import jax
import jax.numpy as jnp
from jax import lax
from jax.experimental import pallas as pl
from jax.experimental.pallas import tpu as pltpu

f32 = jnp.float32
bf16 = jnp.bfloat16
HIGHEST = lax.Precision.HIGHEST

D_MODEL = 1024
D_A = D_MODEL // 2
DH_A = 64
H_A = D_A // DH_A
MOBA_BLOCK = 256
MOBA_TOPK = 3
N_BUCKETS = 32
MAX_DISTANCE = 1024
D_B = D_MODEL // 2
DK = 128
DV = 128
H_B = D_B // DV
CONV_W = 4
CONV_DIM = 2 * H_B * DK + H_B * DV
DELTA_CHUNK = 64
PAGE_SIZE = 128
EPS = 1e-6

LANES = 128
SUBLANES = 8
NEG = -1e30
LOG2E = math.log2(math.e)
TOKEN_TILE = MOBA_BLOCK
SAMPLE_PAD = SUBLANES
NEAR_TILES = MAX_DISTANCE // MOBA_BLOCK + 1
MOBA_HEAD_GROUP = 8
OUTPROJ_BLOCKS = 2
DEN_ROWS = 2 * SUBLANES
W_MAIN = 4 * D_A + CONV_DIM + H_B * DV
W_BA = W_MAIN
W_GATES = W_MAIN + LANES
W_TOTAL = W_GATES + 2 * D_MODEL
VMEM_LIMIT = 48 * 1024 * 1024


def _cparams(*sem):
    return pltpu.CompilerParams(dimension_semantics=sem, vmem_limit_bytes=VMEM_LIMIT)


def _dot(a, b):
    return jnp.dot(a.astype(bf16), b.astype(bf16), preferred_element_type=f32)


def _dot_nt(a, b):
    return lax.dot_general(a.astype(bf16), b.astype(bf16), (((1,), (1,)), ((), ())), preferred_element_type=f32)


def _dot_tn(a, b):
    return lax.dot_general(a.astype(bf16), b.astype(bf16), (((0,), (0,)), ((), ())), preferred_element_type=f32)


def _bmm(a, b):
    return lax.dot_general(a.astype(bf16), b.astype(bf16), (((2,), (1,)), ((0,), (0,))), preferred_element_type=f32)


def _bmm_nt(a, b):
    return lax.dot_general(a.astype(bf16), b.astype(bf16), (((2,), (2,)), ((0,), (0,))), preferred_element_type=f32)


def _bmm_tn(a, b):
    return lax.dot_general(a.astype(bf16), b.astype(bf16), (((1,), (1,)), ((0,), (0,))), preferred_element_type=f32)


def _silu(x):
    return x * jax.nn.sigmoid(x)


def _softplus(x):
    return jnp.maximum(x, 0.0) + jnp.log(1.0 + jnp.exp(-jnp.abs(x)))


def _div_pow2(x, n):
    return lax.shift_right_logical(x, int(math.log2(n)))


def _rel_bucket(dist):
    n = jnp.maximum(dist, 0)
    max_exact = N_BUCKETS // 2
    nf = jnp.maximum(n, 1).astype(f32)
    large = max_exact + (jnp.log(nf / max_exact) / math.log(MAX_DISTANCE / max_exact)
                         * (N_BUCKETS - max_exact)).astype(jnp.int32)
    large = jnp.minimum(large, N_BUCKETS - 1)
    return jnp.where(n < max_exact, n, large)


def _bias_prompt_kernel(rb_ref, o_ref):
    d = pl.program_id(0)
    h = pl.program_id(1)
    r = lax.broadcasted_iota(jnp.int32, (MOBA_BLOCK, MOBA_BLOCK), 0)
    c = lax.broadcasted_iota(jnp.int32, (MOBA_BLOCK, MOBA_BLOCK), 1)
    dist = d * MOBA_BLOCK + c - r
    bucket = _rel_bucket(dist)
    acc = jnp.zeros((MOBA_BLOCK, MOBA_BLOCK), f32)
    for b in range(N_BUCKETS):
        acc = jnp.where(bucket == b, rb_ref[b, h], acc)
    o_ref[...] = jnp.where(dist >= 0, acc * LOG2E, NEG)


def _bias_prompt(rel_bias):
    return pl.pallas_call(
        _bias_prompt_kernel,
        out_shape=jax.ShapeDtypeStruct((NEAR_TILES, H_A, MOBA_BLOCK, MOBA_BLOCK), f32),
        grid=(NEAR_TILES, H_A),
        in_specs=[pl.BlockSpec(memory_space=pltpu.SMEM)],
        out_specs=pl.BlockSpec((None, None, MOBA_BLOCK, MOBA_BLOCK), lambda d, h: (d, h, 0, 0)),
        compiler_params=_cparams("arbitrary", "arbitrary"),
        name="bias_prompt",
    )(rel_bias)


def _bias_sample_kernel(rb_ref, o_ref, far_ref, *, past_len, dec_t):
    rows = H_A * SAMPLE_PAD
    width = past_len + LANES
    row = lax.broadcasted_iota(jnp.int32, (rows, width), 0)
    lane = lax.broadcasted_iota(jnp.int32, (rows, width), 1)
    rowh = _div_pow2(row, SAMPLE_PAD)
    tq = jnp.minimum(row - rowh * SAMPLE_PAD, dec_t - 1)
    tn = lane - past_len
    is_new = lane >= past_len
    dist = jnp.where(is_new, tq - tn, past_len + tq - lane)
    valid = jnp.logical_or(jnp.logical_not(is_new), jnp.logical_and(tn <= tq, tn < dec_t))
    bucket = _rel_bucket(dist)
    far_bucket = _rel_bucket(jnp.full((H_A, LANES), MAX_DISTANCE, jnp.int32))
    far_row = lax.broadcasted_iota(jnp.int32, (H_A, LANES), 0)

    def body(b, carry):
        acc, far = carry
        for h in range(H_A):
            v = rb_ref[b, h]
            acc = jnp.where(jnp.logical_and(bucket == b, rowh == h), v, acc)
            far = jnp.where(jnp.logical_and(far_bucket == b, far_row == h), v, far)
        return acc, far

    acc, far = lax.fori_loop(0, N_BUCKETS, body, (jnp.zeros((rows, width), f32), jnp.zeros((H_A, LANES), f32)))
    o_ref[...] = jnp.where(valid, acc, NEG)
    far_ref[...] = far * LOG2E


def _bias_sample(rel_bias, past_len, dec_t):
    return pl.pallas_call(
        functools.partial(_bias_sample_kernel, past_len=past_len, dec_t=dec_t),
        out_shape=(jax.ShapeDtypeStruct((H_A * SAMPLE_PAD, past_len + LANES), f32),
                   jax.ShapeDtypeStruct((H_A, LANES), f32)),
        in_specs=[pl.BlockSpec(memory_space=pltpu.SMEM)],
        name="bias_sample",
    )(rel_bias)


def _mod_kernel(c_ref, w_ref, b_ref, o_ref):
    o_ref[...] = _dot(_silu(c_ref[...]), w_ref[...]) + b_ref[...]


def _modulation(c_all, w_ada, b_ada):
    depth = w_ada.shape[0]
    rows = c_all.shape[0]
    return pl.pallas_call(
        _mod_kernel,
        out_shape=jax.ShapeDtypeStruct((depth, rows, 3 * D_MODEL), f32),
        grid=(depth, 3),
        in_specs=[pl.BlockSpec((rows, D_MODEL), lambda l, j: (0, 0)),
                  pl.BlockSpec((None, D_MODEL, D_MODEL), lambda l, j: (l, 0, j)),
                  pl.BlockSpec((None, 1, D_MODEL), lambda l, j: (l, 0, j))],
        out_specs=pl.BlockSpec((None, rows, D_MODEL), lambda l, j: (l, 0, j)),
        compiler_params=_cparams("arbitrary", "arbitrary"),
        name="modulation",
    )(c_all, w_ada, b_ada.reshape(depth, 1, 3 * D_MODEL))


def _inproj_kernel(x_ref, shift_ref, scale_ref, gpre_ref, w_ref, *outs, prompt):
    x = x_ref[...]
    h = x * lax.rsqrt(jnp.mean(x * x, axis=-1, keepdims=True) + EPS) * gpre_ref[...]
    hb = (h * (1.0 + scale_ref[...]) + shift_ref[...]).astype(bf16)

    def proj(lo, hi):
        return jnp.dot(hb, w_ref[:, lo:hi], preferred_element_type=f32)

    q = proj(0, D_A)
    k = proj(D_A, 2 * D_A)
    v = proj(2 * D_A, 3 * D_A)
    if prompt:
        k_ref, v_ref, kb_ref, qt_ref, vt_ref, km_ref, za_ref, xb_ref, zb_ref, ba_ref, ga_ref, gb_ref = outs
        vt = v.T
        k_ref[...] = k.T
        v_ref[...] = vt
        kb_ref[...] = k.astype(bf16)
        qt_ref[...] = q.T
        vt_ref[...] = vt.astype(bf16)
        km_ref[...] = jnp.mean(k, axis=0, keepdims=True)
    else:
        q_ref, k_ref, v_ref, za_ref, xb_ref, zb_ref, ba_ref, ga_ref, gb_ref = outs
        q_ref[...] = q
        k_ref[...] = k
        v_ref[...] = v
    za_ref[...] = proj(3 * D_A, 4 * D_A).astype(za_ref.dtype)
    xb_ref[...] = proj(4 * D_A, 4 * D_A + CONV_DIM)
    zb_ref[...] = proj(4 * D_A + CONV_DIM, W_MAIN).astype(zb_ref.dtype)
    ba_ref[...] = proj(W_BA, W_BA + LANES)
    ga_ref[...] = proj(W_GATES, W_GATES + D_MODEL).astype(ga_ref.dtype)
    gb_ref[...] = proj(W_GATES + D_MODEL, W_TOTAL).astype(gb_ref.dtype)


def _inproj(x, shift, scale, g_pre, w_cat, *, prompt, tiles_per_seq=None):
    n = x.shape[0]
    tm = TOKEN_TILE
    nt = n // tm
    row = lambda i: (i, 0)
    if prompt:
        mod_spec = pl.BlockSpec((None, 1, D_MODEL), lambda i: (i // tiles_per_seq, 0, 0))
    else:
        mod_spec = pl.BlockSpec((tm, D_MODEL), row)
    tok = lambda w, dt=f32: (jax.ShapeDtypeStruct((n, w), dt), pl.BlockSpec((tm, w), row))
    gdt = bf16 if prompt else f32
    common = [tok(D_A, gdt), tok(CONV_DIM), tok(D_B, gdt), tok(LANES), tok(D_MODEL, gdt), tok(D_MODEL, gdt)]
    if prompt:
        seq_t = (jax.ShapeDtypeStruct((nt // tiles_per_seq, D_A, tiles_per_seq * tm), f32),
                 pl.BlockSpec((None, D_A, tm), lambda i: (i // tiles_per_seq, 0, i % tiles_per_seq)))
        outs = [seq_t, seq_t, tok(D_A, bf16),
                (jax.ShapeDtypeStruct((nt, D_A, tm), f32), pl.BlockSpec((None, D_A, tm), lambda i: (i, 0, 0))),
                (jax.ShapeDtypeStruct((nt, D_A, tm), bf16), pl.BlockSpec((None, D_A, tm), lambda i: (i, 0, 0))),
                (jax.ShapeDtypeStruct((nt, 1, D_A), f32), pl.BlockSpec((None, 1, D_A), lambda i: (i, 0, 0)))] + common
    else:
        outs = [tok(D_A), tok(D_A), tok(D_A)] + common
    return pl.pallas_call(
        functools.partial(_inproj_kernel, prompt=prompt),
        out_shape=tuple(o[0] for o in outs),
        grid=(nt,),
        in_specs=[pl.BlockSpec((tm, D_MODEL), row), mod_spec, mod_spec,
                  pl.BlockSpec((1, D_MODEL), lambda i: (0, 0)),
                  pl.BlockSpec((D_MODEL, W_TOTAL), lambda i: (0, 0), pipeline_mode=pl.Buffered(1))],
        out_specs=tuple(o[1] for o in outs),
        compiler_params=_cparams("arbitrary"),
        name="inproj_prompt" if prompt else "inproj_sample",
    )(x, shift, scale, g_pre.reshape(1, D_MODEL), w_cat)


def _moba_prompt_kernel(far_ref, qt_ref, kb_ref, vt_ref, km_ref, bias_ref, o_ref, sel_scr, far_scr, qm_scr, *, hg):
    grp = pl.program_id(1)
    i = pl.program_id(2)
    nb = km_ref.shape[0]
    pair_w = 2 * DH_A
    blk = lax.broadcasted_iota(jnp.int32, (nb, MOBA_BLOCK), 0)
    blkf = blk.astype(f32)
    feat = lax.broadcasted_iota(jnp.int32, (pair_w, MOBA_BLOCK), 0)
    for hh in range(hg):
        pair, lo = hh // 2, (hh % 2) * DH_A
        qt = qt_ref[pair * pair_w:(pair + 1) * pair_w, :]
        qm = jnp.where(jnp.logical_and(feat >= lo, feat < lo + DH_A), qt, 0.0)
        gate = jnp.dot(km_ref[:, pair * pair_w:(pair + 1) * pair_w], qm, precision=HIGHEST,
                       preferred_element_type=f32)
        g = jnp.where(blk < i, gate, -jnp.inf)
        chosen = blk == i
        for _ in range(MOBA_TOPK):
            m = jnp.max(g, axis=0, keepdims=True)
            first = jnp.min(jnp.where(g == m, blkf, float(nb)), axis=0, keepdims=True)
            hit = jnp.logical_and(blkf == first, m > -jnp.inf)
            chosen = jnp.logical_or(chosen, hit)
            g = jnp.where(hit, -jnp.inf, g)
        sel_scr[hh] = chosen.astype(f32)
        far_scr[hh] = jnp.full((1, MOBA_BLOCK), far_ref[grp * hg + hh], f32)
        qm_scr[hh] = (qm * (DH_A ** -0.5 * LOG2E)).astype(bf16)

    def step(j, carry, near):
        m, acc = carry
        kj = kb_ref[j]
        kj = jnp.stack([kj[:, (hh // 2) * pair_w:(hh // 2 + 1) * pair_w] for hh in range(hg)])
        vj = jnp.concatenate([vt_ref[j].reshape(hg, DH_A, MOBA_BLOCK),
                              jnp.ones((hg, DEN_ROWS, MOBA_BLOCK), bf16)], axis=1)
        s = _bmm(kj, qm_scr[...])
        if near:
            s = bias_ref[i - j] + s
            shift = 0.0
        else:
            shift = far_scr[...]
        ok = sel_scr[:, pl.ds(j, 1), :] > 0.5
        m_new = jnp.where(ok, jnp.maximum(m, jnp.max(s, axis=1, keepdims=True) + shift), m)
        alpha = jnp.exp2(m - m_new)
        p = jnp.exp2((s - jnp.where(ok, m_new - shift, -NEG)).astype(bf16))
        acc = alpha * acc + _bmm(vj, p)
        return m_new, acc

    init = (jnp.full((hg, 1, MOBA_BLOCK), NEG, f32), jnp.zeros((hg, DH_A + DEN_ROWS, MOBA_BLOCK), f32))
    def sweep(lo, hi, carry, near):
        pairs = lax.shift_right_logical(hi - lo, 1)
        carry = lax.fori_loop(0, pairs, lambda t, c: step(lo + 2 * t + 1, step(lo + 2 * t, c, near), near), carry)
        return lax.fori_loop(lo + 2 * pairs, hi, lambda j, c: step(j, c, near), carry)

    n_far = jnp.maximum(i - (NEAR_TILES - 1), 0)
    _, acc = sweep(n_far, i + 1, sweep(0, n_far, init, False), True)
    o_ref[...] = (acc[:, :DH_A, :] / acc[:, DH_A:DH_A + 1, :]).reshape(hg * DH_A, MOBA_BLOCK)


def _moba_prompt(qt, kb, vt, kmean, bias, far):
    bsz, nb = kb.shape[0], kb.shape[1]
    hg = MOBA_HEAD_GROUP
    gw = hg * DH_A
    return pl.pallas_call(
        functools.partial(_moba_prompt_kernel, hg=hg),
        out_shape=jax.ShapeDtypeStruct((bsz * nb, D_A, MOBA_BLOCK), f32),
        grid=(bsz, H_A // hg, nb),
        in_specs=[pl.BlockSpec(memory_space=pltpu.SMEM),
                  pl.BlockSpec((None, gw, MOBA_BLOCK), lambda b, g, i: (b * nb + i, g, 0)),
                  pl.BlockSpec((None, nb, MOBA_BLOCK, gw), lambda b, g, i: (b, 0, 0, g)),
                  pl.BlockSpec((None, nb, gw, MOBA_BLOCK), lambda b, g, i: (b, 0, g, 0)),
                  pl.BlockSpec((None, nb, gw), lambda b, g, i: (b, 0, g)),
                  pl.BlockSpec((NEAR_TILES, hg, MOBA_BLOCK, MOBA_BLOCK), lambda b, g, i: (0, g, 0, 0))],
        out_specs=pl.BlockSpec((None, gw, MOBA_BLOCK), lambda b, g, i: (b * nb + i, g, 0)),
        scratch_shapes=[pltpu.VMEM((hg, nb, MOBA_BLOCK), f32), pltpu.VMEM((hg, 1, MOBA_BLOCK), f32),
                        pltpu.VMEM((hg, 2 * DH_A, MOBA_BLOCK), bf16)],
        compiler_params=_cparams("arbitrary", "arbitrary", "arbitrary"),
        name="moba_prompt",
    )(far, qt, kb, vt, kmean, bias)


def _moba_sample_kernel(pt_ref, q_ref, kn_ref, vn_ref, bias_ref, *refs, n_pages):
    del pt_ref
    kp = refs[:n_pages]
    vp = refs[n_pages:2 * n_pages]
    o_ref, e_scr, en_scr, l_scr = refs[2 * n_pages:]
    rows = H_A * SAMPLE_PAD
    ppb = MOBA_BLOCK // PAGE_SIZE
    n_blocks = n_pages // ppb
    past = n_pages * PAGE_SIZE
    pad_rows = LANES - SAMPLE_PAD
    step = pl.program_id(0)
    slot = lax.rem(step, 2)
    rowh = _div_pow2(lax.broadcasted_iota(jnp.int32, (rows, D_A), 0), SAMPLE_PAD)
    laneh = _div_pow2(lax.broadcasted_iota(jnp.int32, (rows, D_A), 1), DH_A)
    own = rowh == laneh

    @pl.when(step == 0)
    def _():
        e_scr[1] = jnp.zeros((rows, past), bf16)
        en_scr[1] = jnp.zeros((rows, LANES), bf16)
        l_scr[1] = jnp.ones((rows, LANES), f32)

    vn = jnp.concatenate([vn_ref[...], jnp.zeros((pad_rows, D_A), f32)], axis=0)
    vt = jnp.concatenate([vp[p][...].astype(bf16) for p in range(n_pages)], axis=1)
    acc = lax.dot_general(e_scr[1 - slot], vt, (((1,), (1,)), ((), ())), preferred_element_type=f32)
    acc = acc + jnp.dot(en_scr[1 - slot], vn.astype(bf16), preferred_element_type=f32)
    o = jnp.where(own, acc / l_scr[1 - slot][:, 0:1], 0.0)
    out = o[0:SAMPLE_PAD]
    for h in range(1, H_A):
        out = out + o[h * SAMPLE_PAD:(h + 1) * SAMPLE_PAD]
    o_ref[...] = out

    qe = jnp.where(own, jnp.concatenate([q_ref[...]] * H_A, axis=0), 0.0)
    qeb = (qe * DH_A ** -0.5).astype(bf16)

    lane_f = lax.broadcasted_iota(jnp.int32, (D_A, LANES), 1)
    kmean = jnp.zeros((D_A, LANES), f32)
    for j in range(n_blocks):
        ksum = kp[j * ppb][...]
        for p in range(1, ppb):
            ksum = ksum + kp[j * ppb + p][...]
        kmean = jnp.where(lane_f == j, jnp.sum(ksum, axis=1, keepdims=True) / MOBA_BLOCK, kmean)
    gate = jnp.dot(qe, kmean, precision=HIGHEST, preferred_element_type=f32)
    lane_r = lax.broadcasted_iota(jnp.int32, (rows, LANES), 1)
    lane_rf = lane_r.astype(f32)
    g = jnp.where(lane_r < n_blocks, gate, -jnp.inf)
    chosen = jnp.zeros((rows, LANES), jnp.bool_)
    for _ in range(MOBA_TOPK):
        m = jnp.max(g, axis=1, keepdims=True)
        first = jnp.min(jnp.where(g == m, lane_rf, float(LANES)), axis=1, keepdims=True)
        hit = lane_rf == first
        chosen = jnp.logical_or(chosen, hit)
        g = jnp.where(hit, -jnp.inf, g)
    sel = jnp.where(chosen, 0.0, NEG)

    kn = jnp.concatenate([kn_ref[...], jnp.zeros((pad_rows, D_A), f32)], axis=0)
    kt = jnp.concatenate([kp[p][...].astype(bf16) for p in range(n_pages)], axis=1)
    sel_past = jnp.concatenate([jnp.broadcast_to(sel[:, j:j + 1], (rows, MOBA_BLOCK)) for j in range(n_blocks)],
                               axis=1)
    s_past = bias_ref[:, :past] + sel_past + jnp.dot(qeb, kt, preferred_element_type=f32)
    s_new = _dot_nt(qeb, kn) + bias_ref[:, past:past + LANES]
    m = jnp.maximum(jnp.max(s_past, axis=-1, keepdims=True), jnp.max(s_new, axis=-1, keepdims=True))
    e_past = jnp.exp(s_past - m)
    e_new = jnp.exp(s_new - m)
    l = jnp.sum(e_past, axis=-1, keepdims=True) + jnp.sum(e_new, axis=-1, keepdims=True)
    e_scr[slot] = e_past.astype(bf16)
    en_scr[slot] = e_new.astype(bf16)
    l_scr[slot] = jnp.broadcast_to(l, (rows, LANES))


def _moba_sample(q, kn, vn, cache_k, cache_v, page_table, bias, layer):
    nseq, n_pages = page_table.shape
    assert (n_pages * PAGE_SIZE) % MOBA_BLOCK == 0 and (n_pages * PAGE_SIZE) // MOBA_BLOCK >= MOBA_TOPK
    rows = H_A * SAMPLE_PAD
    cur = lambda b: jnp.minimum(b, nseq - 1)
    prev = lambda b: jnp.maximum(b - 1, 0)
    seq = lambda which: pl.BlockSpec((None, SAMPLE_PAD, D_A), lambda b, pt: (which(b), 0, 0))
    page = lambda which, p: pl.BlockSpec((None, None, D_A, PAGE_SIZE),
                                         lambda b, pt: (layer, pt[which(b), p], 0, 0))
    grid_spec = pltpu.PrefetchScalarGridSpec(
        num_scalar_prefetch=1,
        grid=(nseq + 1,),
        in_specs=[seq(cur), seq(cur), seq(prev), pl.BlockSpec(bias.shape, lambda b, pt: (0, 0))]
        + [page(cur, p) for p in range(n_pages)] + [page(prev, p) for p in range(n_pages)],
        out_specs=seq(prev),
        scratch_shapes=[pltpu.VMEM((2, rows, n_pages * PAGE_SIZE), bf16), pltpu.VMEM((2, rows, LANES), bf16),
                        pltpu.VMEM((2, rows, LANES), f32)],
    )
    return pl.pallas_call(
        functools.partial(_moba_sample_kernel, n_pages=n_pages),
        out_shape=jax.ShapeDtypeStruct((nseq, SAMPLE_PAD, D_A), f32),
        grid_spec=grid_spec,
        compiler_params=_cparams("arbitrary"),
        name="moba_sample",
    )(page_table, q, kn, vn, bias, *([cache_k] * n_pages), *([cache_v] * n_pages))


def _gdn_kernel(x_ref, ba_ref, z_ref, conv0_ref, s0_ref, cw_ref, avec_ref, dtvec_ref, gon_ref,
                o_ref, s_ref, xs_scr, *, chunk, n_valid):
    t = pl.program_id(1)
    nb, tt = x_ref.shape[0], x_ref.shape[1]

    @pl.when(t == 0)
    def _():
        s_ref[...] = s0_ref[...]
        xs_scr[:, 0:SUBLANES, :] = conv0_ref[...]

    xs_scr[:, SUBLANES:SUBLANES + tt, :] = x_ref[...]

    ci = lax.broadcasted_iota(jnp.int32, (chunk, chunk), 0)
    cj = lax.broadcasted_iota(jnp.int32, (chunk, chunk), 1)
    causal = ci >= cj
    strict = ci > cj
    tri = causal.astype(f32)
    eye = (ci == cj).astype(f32)
    cw = cw_ref[...]
    neg_a = -jnp.exp(avec_ref[...])
    merges = []
    for b in range(int(math.log2(chunk))):
        same_pair = lax.shift_right_logical(ci, b + 1) == lax.shift_right_logical(cj, b + 1)
        row_hi = jnp.bitwise_and(lax.shift_right_logical(ci, b), 1) == 1
        col_lo = jnp.bitwise_and(lax.shift_right_logical(cj, b), 1) == 0
        merges.append(jnp.logical_and(same_pair, jnp.logical_and(row_hi, col_lo)))

    nc = tt // chunk
    qs, ks, vs, zs, bcols, gcols = [], [], [], [], [], []
    for n in range(nb):
        slab = xs_scr[n]
        u = pltpu.roll(slab, CONV_W - 1, 0)[SUBLANES:, :] * cw[0:1]
        for j in range(1, CONV_W - 1):
            u = u + pltpu.roll(slab, CONV_W - 1 - j, 0)[SUBLANES:, :] * cw[j:j + 1]
        u = _silu(u + slab[SUBLANES:, :] * cw[CONV_W - 1:CONV_W])
        ba = ba_ref[n]
        beta = jax.nn.sigmoid(ba)
        g = neg_a * _softplus(ba + dtvec_ref[...])
        if n_valid < tt:
            keep = lax.broadcasted_iota(jnp.int32, (tt, LANES), 0) < n_valid
            beta = jnp.where(keep, beta, 0.0)
            g = jnp.where(keep, g, 0.0)
        for c in range(nc):
            rows = slice(c * chunk, (c + 1) * chunk)
            gcum = jnp.dot(tri, g[rows], precision=HIGHEST, preferred_element_type=f32)
            for h in range(H_B):
                qs.append(u[rows, h * DK:(h + 1) * DK])
                ks.append(u[rows, H_B * DK + h * DK:H_B * DK + (h + 1) * DK])
                vs.append(u[rows, 2 * H_B * DK + h * DV:2 * H_B * DK + (h + 1) * DV])
                zs.append(z_ref[n, rows, h * DV:(h + 1) * DV])
                bcols.append(beta[rows, h:h + 1])
                gcols.append(gcum[:, H_B + h:H_B + h + 1])
    q, k, v, z = jnp.stack(qs), jnp.stack(ks), jnp.stack(vs), jnp.stack(zs)
    bcol, gcol = jnp.stack(bcols), jnp.stack(gcols)
    q = q * (lax.rsqrt(jnp.sum(q * q, axis=-1, keepdims=True) + EPS) * DK ** -0.5)
    k = k * lax.rsqrt(jnp.sum(k * k, axis=-1, keepdims=True) + EPS)
    grow = jnp.sum(eye[None] * gcol, axis=1, keepdims=True)
    decay = jnp.exp(jnp.where(causal[None], gcol - grow, NEG))
    kb = k * bcol
    lower = jnp.where(strict[None], _bmm_nt(kb, k) * decay, 0.0)
    inv = eye[None] - jnp.where(merges[0][None], lower, 0.0)
    for merge in merges[1:]:
        inv = inv - _bmm(_bmm(inv, jnp.where(merge[None], lower, 0.0)), inv)
    sol = _bmm(inv, jnp.concatenate([v * bcol, kb * jnp.exp(gcol)], axis=2))
    uu, ww = sol[:, :, :DV], sol[:, :, DV:]
    qk = jnp.where(causal[None], _bmm_nt(q, k) * decay, 0.0)
    qe = q * jnp.exp(gcol)
    g_last = gcol[:, chunk - 1:chunk, :]
    ke = k * jnp.exp(g_last - gcol)
    carry = jnp.exp(g_last)

    def at_chunk(a, c):
        return a.reshape((nb, nc, H_B) + a.shape[1:])[:, c].reshape((nb * H_B,) + a.shape[1:])

    state = s_ref[...].reshape(nb * H_B, DK, DV)
    for c in range(nc):
        v_new = at_chunk(uu, c) - _bmm(at_chunk(ww, c), state)
        o = _bmm(at_chunk(qe, c), state) + _bmm(at_chunk(qk, c), v_new)
        state = state * at_chunk(carry, c) + _bmm_tn(at_chunk(ke, c), v_new)
        o = o * lax.rsqrt(jnp.mean(o * o, axis=-1, keepdims=True) + EPS) * gon_ref[...]
        o = o * _silu(at_chunk(z, c))
        for n in range(nb):
            for h in range(H_B):
                o_ref[n, c * chunk:(c + 1) * chunk, h * DV:(h + 1) * DV] = o[n * H_B + h].astype(o_ref.dtype)
    s_ref[...] = state.reshape(nb, H_B, DK, DV)

    xs_scr[:, 0:SUBLANES, :] = xs_scr[:, tt:tt + SUBLANES, :]


def _gdn(xb, ba, zb, conv0, s0, layer, conv_w, a_log, dt_bias, g_onorm, *, nb, tt, chunk, n_valid):
    nseq, t_len = xb.shape[0], xb.shape[1]
    lane_vec = lambda v: jnp.zeros((1, LANES), f32).at[0, H_B:2 * H_B].set(v)
    tok = lambda w: pl.BlockSpec((nb, tt, w), lambda g, t: (g, t, 0))
    const = lambda shape: pl.BlockSpec(shape, lambda g, t: (0,) * len(shape))
    return pl.pallas_call(
        functools.partial(_gdn_kernel, chunk=chunk, n_valid=n_valid),
        out_shape=(jax.ShapeDtypeStruct((nseq, t_len, D_B), zb.dtype),
                   jax.ShapeDtypeStruct((nseq, H_B, DK, DV), f32)),
        grid=(nseq // nb, t_len // tt),
        in_specs=[tok(CONV_DIM), tok(LANES), tok(D_B),
                  pl.BlockSpec((None, nb, SUBLANES, CONV_DIM), lambda g, t: (layer, g, 0, 0)),
                  pl.BlockSpec((None, nb, H_B, DK, DV), lambda g, t: (layer, g, 0, 0, 0)),
                  const((CONV_W, CONV_DIM)), const((1, LANES)), const((1, LANES)), const((1, DV))],
        out_specs=(tok(D_B), pl.BlockSpec((nb, H_B, DK, DV), lambda g, t: (g, 0, 0, 0))),
        scratch_shapes=[pltpu.VMEM((nb, tt + SUBLANES, CONV_DIM), f32)],
        compiler_params=_cparams("arbitrary", "arbitrary"),
        name="gdn_%d" % tt,
    )(xb, ba, zb, conv0, s0, conv_w, lane_vec(a_log), lane_vec(dt_bias), g_onorm.reshape(1, DV))


def _outproj_kernel(oa_ref, za_ref, ob_ref, ga_ref, gb_ref, x_ref, gate_ref, wpa_ref, wpb_ref, wout_ref,
                    gpost_ref, y_ref, *, prompt):
    if prompt:
        oa = jnp.concatenate([oa_ref[t].T for t in range(oa_ref.shape[0])], axis=0)
    else:
        oa = oa_ref[...]
    oa = oa * _silu(za_ref[...])
    merged = (jax.nn.sigmoid(ga_ref[...]) * _dot(oa, wpa_ref[...])
              + jax.nn.sigmoid(gb_ref[...]) * _dot(ob_ref[...], wpb_ref[...]))
    y = _dot(merged, wout_ref[...])
    y = y * lax.rsqrt(jnp.mean(y * y, axis=-1, keepdims=True) + EPS) * gpost_ref[...]
    y_ref[...] = x_ref[...] + gate_ref[...] * y


def _outproj(oa, za, ob, ga, gb, x, gate, w_pa, w_pb, w_out, g_post, *, prompt, tiles_per_seq=None):
    n = x.shape[0]
    blocks = OUTPROJ_BLOCKS
    tm = blocks * TOKEN_TILE
    assert n % tm == 0 and (tiles_per_seq is None or tiles_per_seq % blocks == 0)
    row = lambda i: (i, 0)
    tok = lambda w: pl.BlockSpec((tm, w), row)
    const = lambda shape: pl.BlockSpec(shape, lambda i: (0, 0))
    if prompt:
        oa_spec = pl.BlockSpec((blocks, D_A, MOBA_BLOCK), lambda i: (i, 0, 0))
        gate_spec = pl.BlockSpec((None, 1, D_MODEL), lambda i: (i // (tiles_per_seq // blocks), 0, 0))
    else:
        oa_spec = tok(D_A)
        gate_spec = tok(D_MODEL)
    return pl.pallas_call(
        functools.partial(_outproj_kernel, prompt=prompt),
        out_shape=jax.ShapeDtypeStruct((n, D_MODEL), f32),
        grid=(n // tm,),
        in_specs=[oa_spec, tok(D_A), tok(D_B), tok(D_MODEL), tok(D_MODEL), tok(D_MODEL), gate_spec,
                  const((D_A, D_MODEL)), const((D_B, D_MODEL)), const((D_MODEL, D_MODEL)), const((1, D_MODEL))],
        out_specs=tok(D_MODEL),
        compiler_params=_cparams("arbitrary"),
        name="outproj_prompt" if prompt else "outproj_sample",
    )(oa, za, ob, ga, gb, x, gate, w_pa, w_pb, w_out, g_post.reshape(1, D_MODEL))


def _pack_w_in(w_in):
    depth = w_in.shape[0]
    n_ba = 2 * H_B
    ba = jnp.zeros((depth, D_MODEL, LANES), w_in.dtype).at[:, :, :n_ba].set(w_in[:, :, W_MAIN:W_MAIN + n_ba])
    return jnp.concatenate([w_in[:, :, :W_MAIN], ba, w_in[:, :, W_MAIN + n_ba:]], axis=-1).astype(bf16)


def _prompt_layer(x, mod, lw, bias, far, zero_conv, zero_state):
    bsz, t_len, _ = x.shape
    n = bsz * t_len
    nb = t_len // MOBA_BLOCK
    g_pre, g_post, w_cat, conv_w, a_log, dt_bias, g_onorm, w_pa, w_pb, w_out = lw
    shift, scale, gate = (mod[:, None, j * D_MODEL:(j + 1) * D_MODEL] for j in range(3))
    xf = x.reshape(n, D_MODEL)
    k, v, kb, qt, vt, km, za, xb, zb, ba, ga, gb = _inproj(xf, shift, scale, g_pre, w_cat, prompt=True,
                                                             tiles_per_seq=nb)
    ot = _moba_prompt(qt, kb.reshape(bsz, nb, MOBA_BLOCK, D_A), vt.reshape(bsz, nb, D_A, MOBA_BLOCK),
                      km.reshape(bsz, nb, D_A), bias, far)
    xb3 = xb.reshape(bsz, t_len, CONV_DIM)
    ob, s_new = _gdn(xb3, ba.reshape(bsz, t_len, LANES), zb.reshape(bsz, t_len, D_B), zero_conv, zero_state, 0,
                     conv_w, a_log, dt_bias, g_onorm, nb=math.gcd(bsz, 4), tt=TOKEN_TILE,
                     chunk=math.gcd(t_len, DELTA_CHUNK),
                     n_valid=TOKEN_TILE)
    y = _outproj(ot, za, ob.reshape(n, D_B), ga, gb, xf, gate, w_pa, w_pb, w_out, g_post, prompt=True,
                 tiles_per_seq=nb)
    heads = lambda a: jnp.transpose(a.reshape(bsz, H_A, DH_A, t_len), (0, 3, 1, 2))
    return y.reshape(bsz, t_len, D_MODEL), heads(k), heads(v), s_new, xb3[:, t_len - (CONV_W - 1):]


def _sample_layer(x, mod, lw, bias, cache_k, cache_v, page_table, state_delta, conv_pad, layer, dec_t):
    nseq = x.shape[0]
    n = nseq * SAMPLE_PAD
    g_pre, g_post, w_cat, conv_w, a_log, dt_bias, g_onorm, w_pa, w_pb, w_out = lw
    per_tok = jnp.repeat(mod, SAMPLE_PAD, axis=0)
    shift, scale, gate = (per_tok[:, j * D_MODEL:(j + 1) * D_MODEL] for j in range(3))
    xf = x.reshape(n, D_MODEL)
    q, k, v, za, xb, zb, ba, ga, gb = _inproj(xf, shift, scale, g_pre, w_cat, prompt=False)
    seq3 = lambda a: a.reshape(nseq, SAMPLE_PAD, a.shape[-1])
    oa = _moba_sample(seq3(q), seq3(k), seq3(v), cache_k, cache_v, page_table, bias, layer)
    xb3 = seq3(xb)
    ob, s_new = _gdn(xb3, seq3(ba), seq3(zb), conv_pad, state_delta, layer, conv_w, a_log, dt_bias, g_onorm,
                     nb=SUBLANES, tt=SAMPLE_PAD, chunk=SAMPLE_PAD, n_valid=dec_t)
    y = _outproj(oa.reshape(n, D_A), za, ob.reshape(n, D_B), ga, gb, xf, gate, w_pa, w_pb, w_out, g_post,
                 prompt=False)
    heads = lambda a: seq3(a)[:, :dec_t].reshape(nseq, dec_t, H_A, DH_A)
    return y.reshape(nseq, SAMPLE_PAD, D_MODEL), heads(k), heads(v), s_new, xb3[:, dec_t - (CONV_W - 1):dec_t]


def kernel(x_prompt, x_sample, c_prompt, c_sample, cache_k, cache_v, state_delta, state_conv, page_table, rel_bias, w_ada, b_ada, g_pre, g_post, w_in, conv_w, a_log, dt_bias, g_onorm, w_pa, w_pb, w_out):
    bsz, seq, _ = x_prompt.shape
    dec_b, dec_t, _ = x_sample.shape
    depth = w_in.shape[0]
    n_pages = page_table.shape[1]
    past_len = n_pages * PAGE_SIZE
    assert seq % MOBA_BLOCK == 0 and seq >= CONV_W - 1 and CONV_W - 1 <= dec_t <= SAMPLE_PAD
    assert dec_b % SUBLANES == 0

    bias_p = _bias_prompt(rel_bias)
    bias_s, far = _bias_sample(rel_bias, past_len, dec_t)
    far = far[:, 0]
    c_all = jnp.concatenate([c_prompt, c_sample], axis=0)
    c_all = jnp.pad(c_all, ((0, -c_all.shape[0] % (2 * SUBLANES)), (0, 0)))
    mod = _modulation(c_all, w_ada, b_ada)
    w_cat = _pack_w_in(w_in)
    w_pa_b, w_pb_b, w_out_b = w_pa.astype(bf16), w_pb.astype(bf16), w_out.astype(bf16)
    ck = jnp.transpose(cache_k, (0, 1, 3, 4, 2)).reshape(cache_k.shape[0], cache_k.shape[1], D_A, PAGE_SIZE)
    cv = jnp.transpose(cache_v, (0, 1, 3, 4, 2)).reshape(cache_v.shape[0], cache_v.shape[1], D_A, PAGE_SIZE)
    conv_pad = jnp.pad(state_conv, ((0, 0), (0, 0), (SUBLANES - (CONV_W - 1), 0), (0, 0)))
    zero_conv = jnp.zeros((1, bsz, SUBLANES, CONV_DIM), f32)
    zero_state = jnp.zeros((1, bsz, H_B, DK, DV), f32)

    hp = x_prompt
    hs = jnp.pad(x_sample, ((0, 0), (0, SAMPLE_PAD - dec_t), (0, 0)))
    outs = [[] for _ in range(8)]
    for l in range(depth):
        lw = (g_pre[l], g_post[l], w_cat[l], conv_w[l], a_log[l], dt_bias[l], g_onorm[l],
              w_pa_b[l], w_pb_b[l], w_out_b[l])
        hp, kp, vp, sp, cp = _prompt_layer(hp, mod[l, :bsz], lw, bias_p, far, zero_conv, zero_state)
        hs, ks, vs, ss, cs = _sample_layer(hs, mod[l, bsz:bsz + dec_b], lw, bias_s, ck, cv, page_table, state_delta,
                                           conv_pad, l, dec_t)
        for lst, val in zip(outs, (kp, vp, ks, vs, sp, ss, cp, cs)):
            lst.append(val)
    return (hp, hs[:, :dec_t]) + tuple(jnp.stack(o) for o in outs)
```
